```python
import math
import jax, jax.numpy as jnp
from jax import lax
import numpy as np

D_MODEL = 1024
BATCH = 16
SEQ = 2048
DEPTH = 2

SSD_HEADS = 16
SSD_HEAD_DIM = 64
SSD_INNER = SSD_HEADS * SSD_HEAD_DIM
SSD_GROUPS = 2
SSD_STATE = 128
SSD_CHUNK = 256
SSD_XBC = SSD_INNER + 2 * SSD_GROUPS * SSD_STATE
CONV_WIDTH = 4
LRU_WIDTH = 1024
LRU_HEADS = 16
LRU_HEAD_DIM = LRU_WIDTH // LRU_HEADS
LRU_C = 8.0
S5_WIDTH = 1024
S5_GROUP = 16
S5_GROUPS = S5_WIDTH // S5_GROUP
S5_STATE = 64
FFN_DENSE = 2816
N_EXPERTS = 8
TOP_K = 2
FFN_EXPERT = 3584
N_DENSE = (DEPTH + 1) // 2
N_MOE = DEPTH // 2
N_BRANCH = 3
EPS = 1e-6

COL_SSD_Z = SSD_INNER
COL_SSD_XBC = SSD_XBC
COL_SSD_DT = SSD_HEADS
COL_LRU_GATE = LRU_WIDTH
COL_LRU_X = LRU_WIDTH
COL_S5_U = S5_WIDTH
COL_GATES = N_BRANCH * D_MODEL
IN_COLS = COL_SSD_Z + COL_SSD_XBC + COL_SSD_DT + COL_LRU_GATE + COL_LRU_X + COL_S5_U + COL_GATES
SPLIT_1 = COL_SSD_Z
SPLIT_2 = SPLIT_1 + COL_SSD_XBC
SPLIT_3 = SPLIT_2 + COL_SSD_DT
SPLIT_4 = SPLIT_3 + COL_LRU_GATE
SPLIT_5 = SPLIT_4 + COL_LRU_X
SPLIT_6 = SPLIT_5 + COL_S5_U

kernel_name = "hybrid_ssd_rglru_s5_moe_block"


def rmsnorm(x, g):
    xf = x.astype(jnp.float32)
    y = xf * lax.rsqrt(jnp.mean(xf * xf, axis=-1, keepdims=True) + EPS)
    return (y * g.astype(jnp.float32)).astype(x.dtype)


def causal_depthwise_conv(x, w, b):
    k, c = w.shape
    y = lax.conv_general_dilated(
        x, w[:, None, :].astype(x.dtype), window_strides=(1,), padding=[(k - 1, 0)],
        dimension_numbers=("NWC", "WIO", "NWC"), feature_group_count=c)
    return y + b.astype(x.dtype)


def segsum_exp(a):
    t = a.shape[-1]
    cs = jnp.cumsum(a, axis=-1)
    diff = cs[..., :, None] - cs[..., None, :]
    mask = jnp.tril(jnp.ones((t, t), dtype=bool))
    return jnp.exp(jnp.where(mask, diff, -jnp.inf))


def linear_recurrence(a, u):
    def combine(l, r):
        a_l, u_l = l
        a_r, u_r = r
        return a_l * a_r, a_r * u_l + u_r
    _, h = lax.associative_scan(combine, (a, u), axis=1)
    return h


def complex_linear_recurrence(a_re, a_im, u_re, u_im):
    def combine(l, r):
        alr, ali, ulr, uli = l
        arr, ari, urr, uri = r
        return (alr * arr - ali * ari,
                alr * ari + ali * arr,
                arr * ulr - ari * uli + urr,
                arr * uli + ari * ulr + uri)
    _, _, h_re, h_im = lax.associative_scan(combine, (a_re, a_im, u_re, u_im), axis=1)
    return h_re, h_im


def ssd_mixer(z, xbc, dt_raw, conv_w, conv_b, dt_bias, a_log, d_skip, norm_g):
    f32 = jnp.float32
    b, s, _ = z.shape
    g_, e_, p_, n_ = SSD_GROUPS, SSD_HEADS // SSD_GROUPS, SSD_HEAD_DIM, SSD_STATE
    q = math.gcd(s, SSD_CHUNK)
    nc = s // q
    xbc = jax.nn.silu(causal_depthwise_conv(xbc, conv_w, conv_b)).astype(f32)
    xs, bm, cm = jnp.split(xbc, [SSD_INNER, SSD_INNER + g_ * n_], axis=-1)
    dt = jax.nn.softplus(dt_raw.astype(f32) + dt_bias.astype(f32))
    a = -jnp.exp(a_log.astype(f32))
    xh = xs.reshape(b, nc, q, g_, e_, p_)
    dtc = dt.reshape(b, nc, q, g_, e_)
    xdt = xh * dtc[..., None]
    bm = bm.reshape(b, nc, q, g_, n_)
    cm = cm.reshape(b, nc, q, g_, n_)
    da = (dtc * a.reshape(g_, e_)).transpose(0, 1, 3, 4, 2)
    da_cs = jnp.cumsum(da, axis=-1)
    lmat = segsum_exp(da)
    cb = jnp.einsum("bclgn,bcsgn->bcgls", cm, bm)
    y_diag = jnp.einsum("bcgels,bcsgep->bclgep", cb[:, :, :, None] * lmat, xdt)
    decay_states = jnp.exp(da_cs[..., -1:] - da_cs)
    states = jnp.einsum("bclgn,bcgel,bclgep->bcgepn", bm, decay_states, xdt)
    chunk_tot = jnp.pad(da_cs[..., -1].transpose(0, 2, 3, 1), [(0, 0), (0, 0), (0, 0), (1, 0)])
    decay_chunk = segsum_exp(chunk_tot)
    states = jnp.concatenate([jnp.zeros_like(states[:, :1]), states], axis=1)
    new_states = jnp.einsum("bgezc,bcgepn->bzgepn", decay_chunk, states)
    states_in = new_states[:, :-1]
    y_off = jnp.einsum("bclgn,bcgepn,bcgel->bclgep", cm, states_in, jnp.exp(da_cs))
    y = y_diag + y_off + xh * d_skip.astype(f32).reshape(g_, e_)[:, :, None]
    y = y.reshape(b, s, SSD_INNER) * jax.nn.silu(z.astype(f32))
    yg = y.reshape(b, s, g_, SSD_INNER // g_)
    yg = yg * lax.rsqrt(jnp.mean(yg * yg, axis=-1, keepdims=True) + EPS)
    return (yg.reshape(b, s, SSD_INNER) * norm_g.astype(f32)).astype(z.dtype)


def rglru_mixer(gate_in, x_in, conv_w, conv_b, w_a, b_a, w_x, b_x, lam):
    f32 = jnp.float32
    b, s, _ = x_in.shape
    gate = jax.nn.gelu(gate_in)
    xc = causal_depthwise_conv(x_in, conv_w, conv_b)
    xh = xc.reshape(b, s, LRU_HEADS, LRU_HEAD_DIM)
    r = jax.nn.sigmoid(jnp.einsum("bshi,hij->bshj", xh, w_a.astype(xc.dtype)).reshape(b, s, LRU_WIDTH).astype(f32) + b_a.astype(f32))
    i = jax.nn.sigmoid(jnp.einsum("bshi,hij->bshj", xh, w_x.astype(xc.dtype)).reshape(b, s, LRU_WIDTH).astype(f32) + b_x.astype(f32))
    log_a = -LRU_C * r * jax.nn.softplus(-lam.astype(f32))
    a = jnp.exp(log_a)
    u = jnp.sqrt(-jnp.expm1(2.0 * log_a)) * (i * xc.astype(f32))
    h = linear_recurrence(a, u)
    return (h * gate.astype(f32)).astype(x_in.dtype)


def s5_mixer(u, lam_re, lam_im, b_re, b_im, c_re, c_im, d_skip, log_dt):
    f32 = jnp.float32
    b, s, _ = u.shape
    lr, li = lam_re.astype(f32), lam_im.astype(f32)
    dt = jnp.exp(log_dt.astype(f32))[:, None]
    mag = jnp.exp(lr * dt)
    abar_re, abar_im = mag * jnp.cos(li * dt), mag * jnp.sin(li * dt)
    den = lr * lr + li * li
    nr, ni = abar_re - 1.0, abar_im
    q_re, q_im = (nr * lr + ni * li) / den, (ni * lr - nr * li) / den
    br, bi = b_re.astype(f32), b_im.astype(f32)
    bb_re = q_re[..., None] * br - q_im[..., None] * bi
    bb_im = q_re[..., None] * bi + q_im[..., None] * br
    ug = u.reshape(b, s, S5_GROUPS, S5_GROUP).astype(f32)
    bu_re = jnp.einsum("bsgh,gph->bsgp", ug, bb_re)
    bu_im = jnp.einsum("bsgh,gph->bsgp", ug, bb_im)
    a_re = jnp.broadcast_to(abar_re[None, None], (1, s, S5_GROUPS, S5_STATE))
    a_im = jnp.broadcast_to(abar_im[None, None], (1, s, S5_GROUPS, S5_STATE))
    h_re, h_im = complex_linear_recurrence(a_re, a_im, bu_re, bu_im)
    y = (jnp.einsum("bsgp,ghp->bsgh", h_re, c_re.astype(f32))
         - jnp.einsum("bsgp,ghp->bsgh", h_im, c_im.astype(f32)))
    y = y + d_skip.astype(f32).reshape(S5_GROUPS, S5_GROUP) * ug
    return jax.nn.gelu(y).reshape(b, s, S5_WIDTH).astype(u.dtype)


def swiglu(x, w_gate, w_up, w_down):
    return (jax.nn.silu(x @ w_gate) * (x @ w_up)) @ w_down


def moe_swiglu(x, w_router, b_router, w_gate, w_up, w_down):
    logits = (x @ w_router).astype(jnp.float32) + b_router.astype(jnp.float32)
    top_vals, top_idx = lax.top_k(logits, TOP_K)
    top_w = jax.nn.softmax(top_vals, axis=-1)
    combine = jnp.sum(jax.nn.one_hot(top_idx, N_EXPERTS, dtype=jnp.float32) * top_w[..., None], axis=-2)
    out = jnp.zeros(x.shape, jnp.float32)
    for e in range(N_EXPERTS):
        out = out + combine[..., e:e + 1] * swiglu(x, w_gate[e], w_up[e], w_down[e]).astype(jnp.float32)
    return out.astype(x.dtype)


def setup_inputs(seed: int = 0) -> dict:
    key = jax.random.key(seed)
    keys = iter(jax.random.split(key, 64))
    f32 = jnp.float32
    L = DEPTH

    def normal(shape, scale):
        return jax.random.normal(next(keys), shape, f32) * scale

    def gain(shape):
        return 1.0 + 0.02 * jax.random.normal(next(keys), shape, f32)

    def uniform(shape, lo, hi):
        return jax.random.uniform(next(keys), shape, f32, lo, hi)

    d = D_MODEL
    x = normal((BATCH, SEQ, d), 1.0)
    mix_norm = gain((L, d))
    w_in = normal((L, d, IN_COLS), d ** -0.5)
    ssd_conv_w = normal((L, CONV_WIDTH, SSD_XBC), CONV_WIDTH ** -0.5)
    ssd_conv_b = normal((L, SSD_XBC), 0.02)
    dt0 = jnp.exp(uniform((L, SSD_HEADS), math.log(1e-3), math.log(1e-1)))
    ssd_dt_bias = dt0 + jnp.log(-jnp.expm1(-dt0))
    ssd_a_log = jnp.log(uniform((L, SSD_HEADS), 1.0, 16.0))
    ssd_d = gain((L, SSD_HEADS))
    ssd_norm = gain((L, SSD_INNER))
    w_branch_ssd = normal((L, SSD_INNER, d), SSD_INNER ** -0.5)
    lru_conv_w = normal((L, CONV_WIDTH, LRU_WIDTH), CONV_WIDTH ** -0.5)
    lru_conv_b = normal((L, LRU_WIDTH), 0.02)
    lru_w_a = normal((L, LRU_HEADS, LRU_HEAD_DIM, LRU_HEAD_DIM), LRU_HEAD_DIM ** -0.5)
    lru_b_a = normal((L, LRU_WIDTH), 0.02)
    lru_w_x = normal((L, LRU_HEADS, LRU_HEAD_DIM, LRU_HEAD_DIM), LRU_HEAD_DIM ** -0.5)
    lru_b_x = normal((L, LRU_WIDTH), 0.02)
    p_a = uniform((L, LRU_WIDTH), 0.9, 0.999) ** (1.0 / LRU_C)
    lru_lambda = jnp.log(p_a) - jnp.log1p(-p_a)
    w_branch_lru = normal((L, LRU_WIDTH, d), LRU_WIDTH ** -0.5)
    n_idx = jnp.arange(S5_STATE, dtype=f32)
    s5_lam_re = -0.5 * (1.0 + 0.01 * normal((L, S5_GROUPS, S5_STATE), 1.0))
    s5_lam_im = math.pi * n_idx * (1.0 + 0.01 * normal((L, S5_GROUPS, S5_STATE), 1.0))
    s5_b_re = normal((L, S5_GROUPS, S5_STATE, S5_GROUP), (0.5 / S5_GROUP) ** 0.5)
    s5_b_im = normal((L, S5_GROUPS, S5_STATE, S5_GROUP), (0.5 / S5_GROUP) ** 0.5)
    s5_c_re = normal((L, S5_GROUPS, S5_GROUP, S5_STATE), (0.5 / S5_STATE) ** 0.5)
    s5_c_im = normal((L, S5_GROUPS, S5_GROUP, S5_STATE), (0.5 / S5_STATE) ** 0.5)
    s5_d = gain((L, S5_WIDTH))
    s5_log_dt = uniform((L, S5_GROUPS), math.log(1e-3), math.log(1e-1))
    w_branch_s5 = normal((L, S5_WIDTH, 2 * d), S5_WIDTH ** -0.5)
    w_out = normal((L, d, d), d ** -0.5)
    ffn_norm = gain((L, d))
    dense_w_gate = normal((N_DENSE, d, FFN_DENSE), d ** -0.5)
    dense_w_up = normal((N_DENSE, d, FFN_DENSE), d ** -0.5)
    dense_w_down = normal((N_DENSE, FFN_DENSE, d), FFN_DENSE ** -0.5)
    moe_router = normal((N_MOE, d, N_EXPERTS), d ** -0.5)
    moe_router_bias = normal((N_MOE, N_EXPERTS), 0.01)
    moe_w_gate = normal((N_MOE, N_EXPERTS, d, FFN_EXPERT), d ** -0.5)
    moe_w_up = normal((N_MOE, N_EXPERTS, d, FFN_EXPERT), d ** -0.5)
    moe_w_down = normal((N_MOE, N_EXPERTS, FFN_EXPERT, d), FFN_EXPERT ** -0.5)
    final_norm = gain((d,))
    return {
        "x": x, "mix_norm": mix_norm, "w_in": w_in,
        "ssd_conv_w": ssd_conv_w, "ssd_conv_b": ssd_conv_b, "ssd_dt_bias": ssd_dt_bias,
        "ssd_a_log": ssd_a_log, "ssd_d": ssd_d, "ssd_norm": ssd_norm, "w_branch_ssd": w_branch_ssd,
        "lru_conv_w": lru_conv_w, "lru_conv_b": lru_conv_b, "lru_w_a": lru_w_a, "lru_b_a": lru_b_a,
        "lru_w_x": lru_w_x, "lru_b_x": lru_b_x, "lru_lambda": lru_lambda, "w_branch_lru": w_branch_lru,
        "s5_lam_re": s5_lam_re, "s5_lam_im": s5_lam_im, "s5_b_re": s5_b_re, "s5_b_im": s5_b_im,
        "s5_c_re": s5_c_re, "s5_c_im": s5_c_im, "s5_d": s5_d, "s5_log_dt": s5_log_dt,
        "w_branch_s5": w_branch_s5, "w_out": w_out, "ffn_norm": ffn_norm,
        "dense_w_gate": dense_w_gate, "dense_w_up": dense_w_up, "dense_w_down": dense_w_down,
        "moe_router": moe_router, "moe_router_bias": moe_router_bias,
        "moe_w_gate": moe_w_gate, "moe_w_up": moe_w_up, "moe_w_down": moe_w_down,
        "final_norm": final_norm,
    }


def reference(x, mix_norm, w_in, ssd_conv_w, ssd_conv_b, ssd_dt_bias, ssd_a_log, ssd_d, ssd_norm,
              w_branch_ssd, lru_conv_w, lru_conv_b, lru_w_a, lru_b_a, lru_w_x, lru_b_x, lru_lambda,
              w_branch_lru, s5_lam_re, s5_lam_im, s5_b_re, s5_b_im, s5_c_re, s5_c_im, s5_d, s5_log_dt,
              w_branch_s5, w_out, ffn_norm, dense_w_gate, dense_w_up, dense_w_down, moe_router,
              moe_router_bias, moe_w_gate, moe_w_up, moe_w_down, final_norm):
    h = x
    b, s, d = x.shape
    for layer in range(DEPTH):
        xn = rmsnorm(h, mix_norm[layer])
        proj = xn @ w_in[layer]
        z, xbc, dt_raw, lru_gate, lru_x, s5_u, gates = jnp.split(
            proj, [SPLIT_1, SPLIT_2, SPLIT_3, SPLIT_4, SPLIT_5, SPLIT_6], axis=-1)
        y_ssd = ssd_mixer(z, xbc, dt_raw, ssd_conv_w[layer], ssd_conv_b[layer], ssd_dt_bias[layer],
                          ssd_a_log[layer], ssd_d[layer], ssd_norm[layer]) @ w_branch_ssd[layer]
        y_lru = rglru_mixer(lru_gate, lru_x, lru_conv_w[layer], lru_conv_b[layer], lru_w_a[layer],
                            lru_b_a[layer], lru_w_x[layer], lru_b_x[layer], lru_lambda[layer]) @ w_branch_lru[layer]
        s5_pre = s5_mixer(s5_u, s5_lam_re[layer], s5_lam_im[layer], s5_b_re[layer], s5_b_im[layer],
                          s5_c_re[layer], s5_c_im[layer], s5_d[layer], s5_log_dt[layer]) @ w_branch_s5[layer]
        s5_val, s5_gate = jnp.split(s5_pre, 2, axis=-1)
        y_s5 = s5_val * jax.nn.sigmoid(s5_gate)
        g = jax.nn.sigmoid(gates.astype(jnp.float32)).reshape(b, s, N_BRANCH, d)
        merged = (g[:, :, 0] * y_ssd.astype(jnp.float32) + g[:, :, 1] * y_lru.astype(jnp.float32)
                  + g[:, :, 2] * y_s5.astype(jnp.float32)).astype(h.dtype)
        h = h + (merged @ w_out[layer]).astype(h.dtype)
        hn = rmsnorm(h, ffn_norm[layer])
        j = layer // 2
        if layer % 2 == 0:
            f = swiglu(hn, dense_w_gate[j], dense_w_up[j], dense_w_down[j])
        else:
            f = moe_swiglu(hn, moe_router[j], moe_router_bias[j], moe_w_gate[j], moe_w_up[j], moe_w_down[j])
        h = h + f.astype(h.dtype)
    return rmsnorm(h, final_norm)
```

```python
import functools
import math

import jax
import jax.numpy as jnp
from jax import lax
from jax.experimental import pallas as pl
from jax.experimental.pallas import tpu as pltpu

F32 = jnp.float32
BF16 = jnp.bfloat16

D_MODEL = 1024
SSD_HEADS = 16
SSD_HEAD_DIM = 64
SSD_INNER = 1024
SSD_GROUPS = 2
SSD_STATE = 128
SSD_CHUNK = 256
CONV_WIDTH = 4
LRU_WIDTH = 1024
LRU_HEADS = 16
LRU_HEAD_DIM = 64
LRU_C = 8.0
S5_WIDTH = 1024
S5_GROUP = 16
S5_GROUPS = 64
S5_STATE = 64
S5_CHUNK = 16
N_EXPERTS = 8
EPS = 1e-6

LANES = 128
CONV_PAD = 8

COL_Z, COL_XS, COL_LG, COL_LX, COL_S5, COL_GATE0 = 0, 1, 2, 3, 4, 5
COL_BC_512 = 16
PROJ_COLS = 8704

VMEM_LIMIT = 52 * 1024 * 1024


def _cparams(*sem):
    return pltpu.CompilerParams(dimension_semantics=sem, vmem_limit_bytes=VMEM_LIMIT)


def _rms(x, g):
    return x * lax.rsqrt(jnp.mean(x * x, axis=-1, keepdims=True) + EPS) * g


def _softplus(x):
    return jnp.maximum(x, 0.0) + jnp.log1p(jnp.exp(-jnp.abs(x)))


def _silu(x):
    return x * jax.nn.sigmoid(x)


def _inproj_kernel(x_ref, g_ref, w_ref, wdt_ref, o_ref, dt_ref, xn_ref):
    @pl.when(pl.program_id(1) == 0)
    def _():
        xn = _rms(x_ref[...], g_ref[...]).astype(BF16)
        xn_ref[...] = xn
        dt_ref[...] = jnp.dot(xn, wdt_ref[...], preferred_element_type=F32)

    o_ref[...] = jnp.dot(xn_ref[...], w_ref[...], preferred_element_type=F32).astype(o_ref.dtype)


def _inproj(h, g, w, wdt, tm=1024, tn=512):
    m = h.shape[0]
    n = w.shape[1]
    return pl.pallas_call(
        _inproj_kernel,
        grid=(m // tm, n // tn),
        in_specs=[
            pl.BlockSpec((tm, D_MODEL), lambda i, j: (i, 0)),
            pl.BlockSpec((1, D_MODEL), lambda i, j: (0, 0)),
            pl.BlockSpec((D_MODEL, tn), lambda i, j: (0, j)),
            pl.BlockSpec((D_MODEL, LANES), lambda i, j: (0, 0)),
        ],
        out_specs=[
            pl.BlockSpec((tm, tn), lambda i, j: (i, j)),
            pl.BlockSpec((tm, LANES), lambda i, j: (i, 0)),
        ],
        out_shape=[
            jax.ShapeDtypeStruct((m, n), BF16),
            jax.ShapeDtypeStruct((m, LANES), F32),
        ],
        scratch_shapes=[pltpu.VMEM((tm, D_MODEL), BF16)],
        compiler_params=_cparams("parallel", "arbitrary"),
        name="inproj",
    )(h, g, w, wdt)


def _conv_chunk(ext_ref, x_f32, w_ref, b_ref, first):
    t = x_f32.shape[0]

    @pl.when(first)
    def _():
        ext_ref[0:CONV_PAD, :] = jnp.zeros((CONV_PAD, ext_ref.shape[1]), F32)

    ext_ref[CONV_PAD:CONV_PAD + t, :] = x_f32
    acc = b_ref[...] + w_ref[CONV_WIDTH - 1:CONV_WIDTH, :] * x_f32
    for k in range(CONV_WIDTH - 1):
        off = CONV_PAD - (CONV_WIDTH - 1) + k
        acc = acc + w_ref[k:k + 1, :] * ext_ref[off:off + t, :]
    ext_ref[CONV_PAD - (CONV_WIDTH - 1):CONV_PAD, :] = ext_ref[CONV_PAD + t - (CONV_WIDTH - 1):CONV_PAD + t, :]
    return acc


def _expand_heads(v, nslab):
    r = v.shape[0]
    lane = lax.broadcasted_iota(jnp.int32, (r, LANES), 1)
    slabs = []
    for k in range(nslab):
        lo = jnp.broadcast_to(v[:, 2 * k:2 * k + 1], (r, LANES))
        hi = jnp.broadcast_to(v[:, 2 * k + 1:2 * k + 2], (r, LANES))
        slabs.append(jnp.where(lane < SSD_HEAD_DIM, lo, hi))
    return jnp.concatenate(slabs, axis=1)


def _ssd_kernel(z_ref, xs_ref, bc_ref, dt_ref, wxs_ref, bxs_ref, wbc_ref, bbc_ref, dtb_ref, alog_ref,
                dskip_ref, ng_ref, o_ref, ext_xs, ext_bc, state_ref):
    q = SSD_CHUNK
    n = SSD_STATE
    gw = SSD_INNER // SSD_GROUPS
    first = pl.program_id(1) == 0

    @pl.when(first)
    def _():
        state_ref[...] = jnp.zeros(state_ref.shape, F32)

    xs = _silu(_conv_chunk(ext_xs, xs_ref[...].astype(F32), wxs_ref, bxs_ref, first))
    bc = _silu(_conv_chunk(ext_bc, bc_ref[...].astype(F32), wbc_ref, bbc_ref, first))

    dt = _softplus(dt_ref[...] + dtb_ref[...])
    a = -jnp.exp(alog_ref[...])
    da = dt * a
    row = lax.broadcasted_iota(jnp.int32, (q, q), 0)
    col = lax.broadcasted_iota(jnp.int32, (q, q), 1)
    causal = row >= col
    tril = causal.astype(F32)
    cs = jnp.dot(tril, da, preferred_element_type=F32, precision=lax.Precision.HIGHEST)
    cs_t = cs.T
    tot = cs[q - 1:q, :]

    xdt = xs * _expand_heads(dt, 8)
    xdt_b = xdt.astype(BF16)
    xdec_b = (xdt * _expand_heads(jnp.exp(tot - cs), 8)).astype(BF16)
    ecs = _expand_heads(jnp.exp(cs), 8)
    etot = _expand_heads(jnp.exp(tot), 8)
    lane = lax.broadcasted_iota(jnp.int32, (q, LANES), 1)

    y_groups = []
    for g in range(SSD_GROUPS):
        bg = bc[:, g * n:(g + 1) * n]
        cg = bc[:, SSD_GROUPS * n + g * n:SSD_GROUPS * n + (g + 1) * n].astype(BF16)
        cb = lax.dot_general(cg, bg.astype(BF16), (((1,), (1,)), ((), ())), preferred_element_type=F32)
        slabs = []
        for k in range(gw // LANES):
            res = []
            for hh in range(2):
                h = g * (SSD_HEADS // SSD_GROUPS) + 2 * k + hh
                seg = cs[:, h:h + 1] - cs_t[h:h + 1, :]
                lmat = jnp.exp(jnp.where(causal, seg, -jnp.inf))
                mh = (cb * lmat).astype(BF16)
                sl = g * gw + k * LANES
                res.append(jnp.dot(mh, xdt_b[:, sl:sl + LANES], preferred_element_type=F32))
            slabs.append(jnp.where(lane < SSD_HEAD_DIM, res[0], res[1]))
        y_diag = jnp.concatenate(slabs, axis=1)
        st = state_ref[g]
        y_off = jnp.dot(cg, st.astype(BF16), preferred_element_type=F32) * ecs[:, g * gw:(g + 1) * gw]
        upd = jnp.dot(bg.T.astype(BF16), xdec_b[:, g * gw:(g + 1) * gw], preferred_element_type=F32)
        state_ref[g] = st * etot[:, g * gw:(g + 1) * gw] + upd
        yg = y_diag + y_off + xs[:, g * gw:(g + 1) * gw] * dskip_ref[:, g * gw:(g + 1) * gw]
        yg = yg * _silu(z_ref[:, g * gw:(g + 1) * gw].astype(F32))
        yg = yg * lax.rsqrt(jnp.mean(yg * yg, axis=-1, keepdims=True) + EPS)
        y_groups.append(yg)
    y = jnp.concatenate(y_groups, axis=1) * ng_ref[...]
    o_ref[...] = y.astype(o_ref.dtype)


def _ssd(proj, dt_raw, wxs, bxs, wbc, bbc, dtb, alog, dskip, ng, batch, seq):
    q = SSD_CHUNK
    nc = seq // q
    m = batch * seq
    full = lambda shape: pl.BlockSpec(shape, lambda b, c: (0,) * len(shape))
    return pl.pallas_call(
        _ssd_kernel,
        grid=(batch, nc),
        in_specs=[
            pl.BlockSpec((q, 1024), lambda b, c: (b * nc + c, COL_Z)),
            pl.BlockSpec((q, 1024), lambda b, c: (b * nc + c, COL_XS)),
            pl.BlockSpec((q, 512), lambda b, c: (b * nc + c, COL_BC_512)),
            pl.BlockSpec((q, LANES), lambda b, c: (b * nc + c, 0)),
            full((CONV_WIDTH, 1024)), full((1, 1024)), full((CONV_WIDTH, 512)), full((1, 512)),
            full((1, LANES)), full((1, LANES)), full((1, 1024)), full((1, 1024)),
        ],
        out_specs=pl.BlockSpec((q, 1024), lambda b, c: (b * nc + c, 0)),
        out_shape=jax.ShapeDtypeStruct((m, SSD_INNER), BF16),
        scratch_shapes=[
            pltpu.VMEM((CONV_PAD + q, 1024), F32),
            pltpu.VMEM((CONV_PAD + q, 512), F32),
            pltpu.VMEM((SSD_GROUPS, SSD_STATE, SSD_INNER // SSD_GROUPS), F32),
        ],
        compiler_params=_cparams("parallel", "arbitrary"),
        name="ssd",
    )(proj, proj, proj, dt_raw, wxs, bxs, wbc, bbc, dtb, alog, dskip, ng)


def _lru_kernel(lg_ref, lx_ref, cw_ref, cb_ref, wbd_ref, ba_ref, bx_ref, lam_ref, o_ref, ext, carry_ref):
    t = lx_ref.shape[0]
    first = pl.program_id(1) == 0

    @pl.when(first)
    def _():
        carry_ref[...] = jnp.zeros(carry_ref.shape, F32)

    xc = _conv_chunk(ext, lx_ref[...].astype(F32), cw_ref, cb_ref, first)
    xcb = xc.astype(BF16)
    nblk = wbd_ref.shape[0]
    bw = LRU_WIDTH // nblk
    r_parts, i_parts = [], []
    for s in range(nblk):
        ga = jnp.dot(xcb[:, s * bw:(s + 1) * bw], wbd_ref[s], preferred_element_type=F32)
        r_parts.append(ga[:, :bw])
        i_parts.append(ga[:, bw:])
    r = jax.nn.sigmoid(jnp.concatenate(r_parts, axis=1) + ba_ref[...])
    i = jax.nn.sigmoid(jnp.concatenate(i_parts, axis=1) + bx_ref[...])
    log_a = (-LRU_C) * r * _softplus(-lam_ref[...])
    a = jnp.exp(log_a)
    u = jnp.sqrt(-jnp.tanh(log_a) * (a * a + 1.0)) * (i * xc)

    rowi = lax.broadcasted_iota(jnp.int32, (t, LRU_WIDTH), 0)
    sh = 1
    while sh < t:
        keep = rowi >= sh
        a_s = jnp.where(keep, pltpu.roll(a, sh, 0), 1.0)
        u_s = jnp.where(keep, pltpu.roll(u, sh, 0), 0.0)
        u = a * u_s + u
        a = a * a_s
        sh *= 2
    h = u + a * carry_ref[...]
    carry_ref[...] = h[t - 1:t, :]
    o_ref[...] = (h * jax.nn.gelu(lg_ref[...].astype(F32))).astype(o_ref.dtype)


def _lru(proj, cw, cb, wbd, ba, bx, lam, batch, seq, tc=256):
    nc = seq // tc
    m = batch * seq
    full = lambda shape: pl.BlockSpec(shape, lambda b, c: (0,) * len(shape))
    return pl.pallas_call(
        _lru_kernel,
        grid=(batch, nc),
        in_specs=[
            pl.BlockSpec((tc, 1024), lambda b, c: (b * nc + c, COL_LG)),
            pl.BlockSpec((tc, 1024), lambda b, c: (b * nc + c, COL_LX)),
            full((CONV_WIDTH, 1024)), full((1, 1024)), full(wbd.shape),
            full((1, 1024)), full((1, 1024)), full((1, 1024)),
        ],
        out_specs=pl.BlockSpec((tc, 1024), lambda b, c: (b * nc + c, 0)),
        out_shape=jax.ShapeDtypeStruct((m, LRU_WIDTH), BF16),
        scratch_shapes=[pltpu.VMEM((CONV_PAD + tc, 1024), F32), pltpu.VMEM((1, LRU_WIDTH), F32)],
        compiler_params=_cparams("parallel", "arbitrary"),
        name="rglru",
    )(proj, proj, cw, cb, wbd, ba, bx, lam)


def _s5_kernel(x_ref, t_ref, bs_ref, co_ref, ar_ref, ai_ref, d_ref, o_ref, *, nchunk):
    x = x_ref[0]
    r = x.shape[0]
    y = jnp.dot(x, t_ref[0], preferred_element_type=F32)
    hs = jnp.dot(x, bs_ref[0], preferred_element_type=F32)
    cidx = lax.broadcasted_iota(jnp.int32, (r, 2 * S5_STATE), 0) % nchunk
    ar = ar_ref[0]
    ai = ai_ref[0]
    j = 0
    sh = 1
    while sh < nchunk:
        prev = jnp.where(cidx >= sh, pltpu.roll(hs, sh, 0), 0.0)
        hs = hs + ar[j:j + 1, :] * prev + ai[j:j + 1, :] * pltpu.roll(prev, S5_STATE, 1)
        sh *= 2
        j += 1
    h_in = jnp.where(cidx >= 1, pltpu.roll(hs, 1, 0), 0.0)
    y = y + jnp.dot(h_in.astype(BF16), co_ref[0], preferred_element_type=F32)
    y = y + x.astype(F32) * d_ref[0]
    o_ref[0] = jax.nn.gelu(y).astype(o_ref.dtype)


def _s5(xg, tmat, bs, co, ar, ai, dflat, nchunk):
    ng, r, w = xg.shape
    per_g = lambda a: pl.BlockSpec((1,) + a.shape[1:], lambda g: (g, 0, 0))
    return pl.pallas_call(
        functools.partial(_s5_kernel, nchunk=nchunk),
        grid=(ng,),
        in_specs=[per_g(xg), per_g(tmat), per_g(bs), per_g(co), per_g(ar), per_g(ai), per_g(dflat)],
        out_specs=pl.BlockSpec((1, r, w), lambda g: (g, 0, 0)),
        out_shape=jax.ShapeDtypeStruct((ng, r, w), BF16),
        compiler_params=_cparams("parallel"),
        name="s5",
    )(xg, tmat, bs, co, ar, ai, dflat)


def _s5_matrices(lam_re, lam_im, b_re, b_im, c_re, c_im, d_skip, log_dt, nchunk):
    q = S5_CHUNK
    g_, p_, h_ = S5_GROUPS, S5_STATE, S5_GROUP
    dt = jnp.exp(log_dt)[:, None]
    lr, li = lam_re, lam_im

    def apow(k):
        kk = k[:, None, None].astype(F32)
        mag = jnp.exp(kk * (lr * dt)[None])
        ang = kk * (li * dt)[None]
        return mag * jnp.cos(ang), mag * jnp.sin(ang)

    abar_re, abar_im = apow(jnp.arange(1, 2))
    abar_re, abar_im = abar_re[0], abar_im[0]
    den = lr * lr + li * li
    nr, ni = abar_re - 1.0, abar_im
    q_re, q_im = (nr * lr + ni * li) / den, (ni * lr - nr * li) / den
    bb_re = q_re[..., None] * b_re - q_im[..., None] * b_im
    bb_im = q_re[..., None] * b_im + q_im[..., None] * b_re
    pw_re, pw_im = apow(jnp.arange(q + 1))
    ca_re = c_re[None] * pw_re[:, :, None, :] - c_im[None] * pw_im[:, :, None, :]
    ca_im = c_re[None] * pw_im[:, :, None, :] + c_im[None] * pw_re[:, :, None, :]
    kk = (jnp.einsum("kgip,gpj->kgij", ca_re[:q], bb_re, precision="highest")
          - jnp.einsum("kgip,gpj->kgij", ca_im[:q], bb_im, precision="highest"))
    s_idx = jnp.arange(q)[:, None]
    t_idx = jnp.arange(q)[None, :]
    lag = t_idx - s_idx
    kt = kk[jnp.clip(lag, 0, q - 1)]
    kt = jnp.where((lag >= 0)[:, :, None, None, None], kt, 0.0)
    tmat = kt.transpose(2, 0, 4, 1, 3).reshape(g_, q * h_, q * h_)
    rp_re, rp_im = pw_re[:q][::-1], pw_im[:q][::-1]
    bs_re = rp_re[..., None] * bb_re[None] - rp_im[..., None] * bb_im[None]
    bs_im = rp_re[..., None] * bb_im[None] + rp_im[..., None] * bb_re[None]
    bs = jnp.concatenate([bs_re, bs_im], axis=2).transpose(1, 0, 3, 2).reshape(g_, q * h_, 2 * p_)
    co_re = ca_re[1:].transpose(1, 3, 0, 2).reshape(g_, p_, q * h_)
    co_im = ca_im[1:].transpose(1, 3, 0, 2).reshape(g_, p_, q * h_)
    co = jnp.concatenate([co_re, -co_im], axis=1)
    nstep = max(1, int(math.ceil(math.log2(nchunk)))) if nchunk > 1 else 1
    sp_re, sp_im = apow(q * (2 ** jnp.arange(nstep)))
    nrow = 8 * ((nstep + 7) // 8)
    ar = jnp.concatenate([sp_re, sp_re], axis=-1).transpose(1, 0, 2)
    ai = jnp.concatenate([-sp_im, sp_im], axis=-1).transpose(1, 0, 2)
    pad = ((0, 0), (0, nrow - nstep), (0, 0))
    dflat = jnp.tile(d_skip.reshape(g_, 1, h_), (1, 1, q))
    return tmat.astype(BF16), bs.astype(BF16), co.astype(BF16), jnp.pad(ar, pad), jnp.pad(ai, pad), dflat


def _merge_kernel(h_ref, ys_ref, yl_ref, y5_ref, g0_ref, g1_ref, g2_ref, ws_ref, wl_ref, w5_ref, wo_ref, o_ref):
    d = D_MODEL
    p_ssd = jnp.dot(ys_ref[...], ws_ref[...], preferred_element_type=F32)
    p_lru = jnp.dot(yl_ref[...], wl_ref[...], preferred_element_type=F32)
    p_s5 = jnp.dot(y5_ref[...], w5_ref[...], preferred_element_type=F32)
    y_s5 = p_s5[:, :d] * jax.nn.sigmoid(p_s5[:, d:])
    merged = (jax.nn.sigmoid(g0_ref[...].astype(F32)) * p_ssd
              + jax.nn.sigmoid(g1_ref[...].astype(F32)) * p_lru
              + jax.nn.sigmoid(g2_ref[...].astype(F32)) * y_s5)
    o_ref[...] = h_ref[...] + jnp.dot(merged.astype(BF16), wo_ref[...], preferred_element_type=F32)


def _merge(h, y_ssd, y_lru, y_s5, proj, w_ssd, w_lru, w_s5, w_out, tm=512):
    m = h.shape[0]
    d = D_MODEL
    row = lambda c: pl.BlockSpec((tm, d), lambda i, c=c: (i, c))
    full = lambda shape: pl.BlockSpec(shape, lambda i: (0, 0))
    return pl.pallas_call(
        _merge_kernel,
        grid=(m // tm,),
        in_specs=[row(0), row(0), row(0), row(0), row(COL_GATE0), row(COL_GATE0 + 1), row(COL_GATE0 + 2),
                  full((d, d)), full((d, d)), full((d, 2 * d)), full((d, d))],
        out_specs=row(0),
        out_shape=jax.ShapeDtypeStruct((m, d), F32),
        compiler_params=_cparams("parallel"),
        name="merge",
    )(h, y_ssd, y_lru, y_s5, proj, proj, proj, w_ssd, w_lru, w_s5, w_out)


def _ffn_kernel(h_ref, g_ref, wg_ref, wu_ref, wd_ref, o_ref, hn_ref, acc_ref):
    f = pl.program_id(1)

    @pl.when(f == 0)
    def _():
        hn_ref[...] = _rms(h_ref[...], g_ref[...]).astype(BF16)
        acc_ref[...] = jnp.zeros(acc_ref.shape, F32)

    hn = hn_ref[...]
    a = jnp.dot(hn, wg_ref[...], preferred_element_type=F32)
    b = jnp.dot(hn, wu_ref[...], preferred_element_type=F32)
    acc_ref[...] += jnp.dot((_silu(a) * b).astype(BF16), wd_ref[...], preferred_element_type=F32)

    @pl.when(f == pl.num_programs(1) - 1)
    def _():
        o_ref[...] = h_ref[...] + acc_ref[...]


def _ffn(h, g, wg, wu, wd, tm=512, tf=1408):
    m = h.shape[0]
    d = D_MODEL
    nf = wg.shape[1] // tf
    return pl.pallas_call(
        _ffn_kernel,
        grid=(m // tm, nf),
        in_specs=[
            pl.BlockSpec((tm, d), lambda i, f: (i, 0)),
            pl.BlockSpec((1, d), lambda i, f: (0, 0)),
            pl.BlockSpec((d, tf), lambda i, f: (0, f)),
            pl.BlockSpec((d, tf), lambda i, f: (0, f)),
            pl.BlockSpec((tf, d), lambda i, f: (f, 0)),
        ],
        out_specs=pl.BlockSpec((tm, d), lambda i, f: (i, 0)),
        out_shape=jax.ShapeDtypeStruct((m, d), F32),
        scratch_shapes=[pltpu.VMEM((tm, d), BF16), pltpu.VMEM((tm, d), F32)],
        compiler_params=_cparams("parallel", "arbitrary"),
        name="ffn_dense",
    )(h, g, wg, wu, wd)


def _router_kernel(h_ref, g_ref, wr_ref, br_ref, o_ref):
    hn = _rms(h_ref[...], g_ref[...])
    logits = jnp.dot(hn, wr_ref[...], preferred_element_type=F32, precision=lax.Precision.HIGHEST) + br_ref[...]
    lane = lax.broadcasted_iota(jnp.int32, logits.shape, 1)
    logits = jnp.where(lane < N_EXPERTS, logits, -jnp.inf)
    m1 = jnp.max(logits, axis=-1, keepdims=True)
    i1 = jnp.min(jnp.where(logits == m1, lane, LANES), axis=-1, keepdims=True)
    rest = jnp.where(lane == i1, -jnp.inf, logits)
    m2 = jnp.max(rest, axis=-1, keepdims=True)
    i2 = jnp.min(jnp.where(rest == m2, lane, LANES), axis=-1, keepdims=True)
    e2 = jnp.exp(m2 - m1)
    w1 = 1.0 / (1.0 + e2)
    w2 = e2 / (1.0 + e2)
    o_ref[...] = jnp.where(lane == i1, w1, 0.0) + jnp.where(lane == i2, w2, 0.0)


def _router(h, g, wr, br, tm=512):
    m = h.shape[0]
    d = D_MODEL
    return pl.pallas_call(
        _router_kernel,
        grid=(m // tm,),
        in_specs=[
            pl.BlockSpec((tm, d), lambda i: (i, 0)),
            pl.BlockSpec((1, d), lambda i: (0, 0)),
            pl.BlockSpec((d, LANES), lambda i: (0, 0)),
            pl.BlockSpec((1, LANES), lambda i: (0, 0)),
        ],
        out_specs=pl.BlockSpec((tm, LANES), lambda i: (i, 0)),
        out_shape=jax.ShapeDtypeStruct((m, LANES), F32),
        compiler_params=_cparams("parallel"),
        name="router",
    )(h, g, wr, br)


def _moe_kernel(h_ref, g_ref, cmb_ref, wg_ref, wu_ref, wd_ref, fg_ref, o_ref, hn_ref, acc_ref, *, final):
    e = pl.program_id(1)
    f = pl.program_id(2)

    @pl.when((e == 0) & (f == 0))
    def _():
        hn_ref[...] = _rms(h_ref[...], g_ref[...]).astype(BF16)
        acc_ref[...] = jnp.zeros(acc_ref.shape, F32)

    hn = hn_ref[...]
    cmb = cmb_ref[...]
    lane = lax.broadcasted_iota(jnp.int32, cmb.shape, 1)
    ce = jnp.sum(jnp.where(lane == e, cmb, 0.0), axis=-1, keepdims=True)
    a = jnp.dot(hn, wg_ref[0], preferred_element_type=F32)
    b = jnp.dot(hn, wu_ref[0], preferred_element_type=F32)
    act = (_silu(a) * b * ce).astype(BF16)
    acc_ref[...] += jnp.dot(act, wd_ref[0], preferred_element_type=F32)

    @pl.when((e == pl.num_programs(1) - 1) & (f == pl.num_programs(2) - 1))
    def _():
        out = h_ref[...] + acc_ref[...]
        o_ref[...] = _rms(out, fg_ref[...]) if final else out


def _moe(h, g, cmb, wg, wu, wd, fg, final, tm=1024, tf=512):
    m = h.shape[0]
    d = D_MODEL
    ne, _, fe = wg.shape
    return pl.pallas_call(
        functools.partial(_moe_kernel, final=final),
        grid=(m // tm, ne, fe // tf),
        in_specs=[
            pl.BlockSpec((tm, d), lambda i, e, f: (i, 0)),
            pl.BlockSpec((1, d), lambda i, e, f: (0, 0)),
            pl.BlockSpec((tm, LANES), lambda i, e, f: (i, 0)),
            pl.BlockSpec((1, d, tf), lambda i, e, f: (e, 0, f)),
            pl.BlockSpec((1, d, tf), lambda i, e, f: (e, 0, f)),
            pl.BlockSpec((1, tf, d), lambda i, e, f: (e, f, 0)),
            pl.BlockSpec((1, d), lambda i, e, f: (0, 0)),
        ],
        out_specs=pl.BlockSpec((tm, d), lambda i, e, f: (i, 0)),
        out_shape=jax.ShapeDtypeStruct((m, d), F32),
        scratch_shapes=[pltpu.VMEM((tm, d), BF16), pltpu.VMEM((tm, d), F32)],
        compiler_params=_cparams("parallel", "arbitrary", "arbitrary"),
        name="moe_experts",
    )(h, g, cmb, wg, wu, wd, fg)


def _final_norm_kernel(h_ref, g_ref, o_ref):
    o_ref[...] = _rms(h_ref[...], g_ref[...])


def _final_norm(h, g, tm=1024):
    m, d = h.shape
    return pl.pallas_call(
        _final_norm_kernel,
        grid=(m // tm,),
        in_specs=[pl.BlockSpec((tm, d), lambda i: (i, 0)), pl.BlockSpec((1, d), lambda i: (0, 0))],
        out_specs=pl.BlockSpec((tm, d), lambda i: (i, 0)),
        out_shape=jax.ShapeDtypeStruct((m, d), F32),
        compiler_params=_cparams("parallel"),
        name="final_norm",
    )(h, g)


def _pack_w_in(w):
    z = w[:, 0:1024]
    xs = w[:, 1024:2048]
    bc = w[:, 2048:2560]
    dt = w[:, 2560:2576]
    lg = w[:, 2576:3600]
    lx = w[:, 3600:4624]
    s5 = w[:, 4624:5648]
    gates = w[:, 5648:8720]
    packed = jnp.concatenate([z, xs, lg, lx, s5, gates, bc], axis=1).astype(BF16)
    wdt = jnp.pad(dt, ((0, 0), (0, LANES - SSD_HEADS))).astype(BF16)
    return packed, wdt


def _block_diag_gates(w_a, w_x, bw=256):
    per = bw // LRU_HEAD_DIM
    nblk = LRU_WIDTH // bw
    eye = jnp.eye(per, dtype=F32)

    def bd(w):
        w = w.reshape(nblk, per, LRU_HEAD_DIM, LRU_HEAD_DIM)
        return jnp.einsum("sbij,bc->sbicj", w, eye).reshape(nblk, bw, bw)

    return jnp.concatenate([bd(w_a), bd(w_x)], axis=2).astype(BF16)


def _row(v, width=None):
    v = v.reshape(1, -1).astype(F32)
    if width is not None and v.shape[1] < width:
        v = jnp.pad(v, ((0, 0), (0, width - v.shape[1])))
    return v


def kernel(x, mix_norm, w_in, ssd_conv_w, ssd_conv_b, ssd_dt_bias, ssd_a_log, ssd_d, ssd_norm, w_branch_ssd, lru_conv_w, lru_conv_b, lru_w_a, lru_b_a, lru_w_x, lru_b_x, lru_lambda, w_branch_lru, s5_lam_re, s5_lam_im, s5_b_re, s5_b_im, s5_c_re, s5_c_im, s5_d, s5_log_dt, w_branch_s5, w_out, ffn_norm, dense_w_gate, dense_w_up, dense_w_down, moe_router, moe_router_bias, moe_w_gate, moe_w_up, moe_w_down, final_norm):
    batch, seq, d = x.shape
    depth = mix_norm.shape[0]
    m = batch * seq
    nchunk5 = seq // S5_CHUNK
    h = x.reshape(m, d)
    normed = False
    for layer in range(depth):
        w_pack, w_dt = _pack_w_in(w_in[layer])
        proj, dt_raw = _inproj(h, _row(mix_norm[layer]), w_pack, w_dt)

        cw = ssd_conv_w[layer]
        cb = ssd_conv_b[layer]
        alog = jnp.pad(ssd_a_log[layer].reshape(1, -1), ((0, 0), (0, LANES - SSD_HEADS)), constant_values=-jnp.inf)
        y_ssd = _ssd(proj, dt_raw, cw[:, :SSD_INNER], _row(cb[:SSD_INNER]), cw[:, SSD_INNER:], _row(cb[SSD_INNER:]),
                     _row(ssd_dt_bias[layer], LANES), alog, _row(jnp.repeat(ssd_d[layer], SSD_HEAD_DIM)),
                     _row(ssd_norm[layer]), batch, seq)

        y_lru = _lru(proj, lru_conv_w[layer], _row(lru_conv_b[layer]),
                     _block_diag_gates(lru_w_a[layer], lru_w_x[layer]), _row(lru_b_a[layer]),
                     _row(lru_b_x[layer]), _row(lru_lambda[layer]), batch, seq)

        tmat, bs, co, ar, ai, dflat = _s5_matrices(
            s5_lam_re[layer], s5_lam_im[layer], s5_b_re[layer], s5_b_im[layer], s5_c_re[layer], s5_c_im[layer],
            s5_d[layer], s5_log_dt[layer], nchunk5)
        u5 = proj[:, COL_S5 * 1024:(COL_S5 + 1) * 1024]
        xg = u5.reshape(m // S5_CHUNK, S5_CHUNK, S5_GROUPS, S5_GROUP).transpose(2, 0, 1, 3)
        xg = xg.reshape(S5_GROUPS, m // S5_CHUNK, S5_CHUNK * S5_GROUP)
        yg = _s5(xg, tmat, bs, co, ar, ai, dflat, nchunk5)
        y_s5 = yg.reshape(S5_GROUPS, m // S5_CHUNK, S5_CHUNK, S5_GROUP).transpose(1, 2, 0, 3).reshape(m, S5_WIDTH)

        h = _merge(h, y_ssd, y_lru, y_s5, proj, w_branch_ssd[layer].astype(BF16), w_branch_lru[layer].astype(BF16),
                   w_branch_s5[layer].astype(BF16), w_out[layer].astype(BF16))

        j = layer // 2
        if layer % 2 == 0:
            h = _ffn(h, _row(ffn_norm[layer]), dense_w_gate[j].astype(BF16), dense_w_up[j].astype(BF16),
                     dense_w_down[j].astype(BF16))
        else:
            gn = _row(ffn_norm[layer])
            wr = jnp.pad(moe_router[j], ((0, 0), (0, LANES - N_EXPERTS)))
            cmb = _router(h, gn, wr, _row(moe_router_bias[j], LANES))
            normed = layer == depth - 1
            h = _moe(h, gn, cmb, moe_w_gate[j].astype(BF16), moe_w_up[j].astype(BF16),
                     moe_w_down[j].astype(BF16), _row(final_norm), normed)
    if not normed:
        h = _final_norm(h, _row(final_norm))
    return h.reshape(batch, seq, d)
```

```python
import functools
import math

import jax
import jax.numpy as jnp
from jax import lax
from jax.experimental import pallas as pl
from jax.experimental.pallas import tpu as pltpu

F32 = jnp.float32
BF16 = jnp.bfloat16

D_MODEL = 1024
SSD_HEADS = 16
SSD_HEAD_DIM = 64
SSD_INNER = 1024
SSD_GROUPS = 2
SSD_STATE = 128
SSD_CHUNK = 256
CONV_WIDTH = 4
LRU_WIDTH = 1024
LRU_HEADS = 16
LRU_HEAD_DIM = 64
LRU_C = 8.0
S5_WIDTH = 1024
S5_GROUP = 16
S5_GROUPS = 64
S5_STATE = 64
S5_CHUNK = 16
S5_SLAB_GROUPS = 8
N_EXPERTS = 8
EPS = 1e-6

LANES = 128
CONV_PAD = 8

COL_Z, COL_XS, COL_LG, COL_LX, COL_GATE0 = 0, 1, 2, 3, 4
COL_BC_512 = 14
INPROJ_TN = 512
S5_TILES = S5_WIDTH // INPROJ_TN

MOE_TM = 1024
MOE_CAP = 320
RANK_BLK = 256

VMEM_LIMIT = 52 * 1024 * 1024
SINGLE = pl.Buffered(1)


def _cparams(*sem):
    return pltpu.CompilerParams(dimension_semantics=sem, vmem_limit_bytes=VMEM_LIMIT)


def _rms(x, g):
    return x * lax.rsqrt(jnp.mean(x * x, axis=-1, keepdims=True) + EPS) * g


def _softplus(x):
    return jnp.maximum(x, 0.0) + jnp.log1p(jnp.exp(-jnp.abs(x)))


def _silu(x):
    return x * jax.nn.sigmoid(x)


def _inproj_kernel(x_ref, g_ref, w_ref, wdt_ref, o_ref, u5_ref, dt_ref, xn_ref):
    j = pl.program_id(1)

    @pl.when(j == 0)
    def _():
        xn = _rms(x_ref[...], g_ref[...]).astype(BF16)
        xn_ref[...] = xn
        dt_ref[...] = jnp.dot(xn, wdt_ref[...], preferred_element_type=F32)

    res = jnp.dot(xn_ref[...], w_ref[...], preferred_element_type=F32)

    @pl.when(j < S5_TILES)
    def _():
        u5_ref[...] = res

    @pl.when(j >= S5_TILES)
    def _():
        o_ref[...] = res.astype(o_ref.dtype)


def _inproj(h, g, w, wdt, tm=2048):
    m = h.shape[0]
    tm = min(tm, m)
    tn = INPROJ_TN
    n = w.shape[1]
    return pl.pallas_call(
        _inproj_kernel,
        grid=(m // tm, n // tn),
        in_specs=[
            pl.BlockSpec((tm, D_MODEL), lambda i, j: (i, 0)),
            pl.BlockSpec((1, D_MODEL), lambda i, j: (0, 0)),
            pl.BlockSpec((D_MODEL, tn), lambda i, j: (0, j)),
            pl.BlockSpec((D_MODEL, LANES), lambda i, j: (0, 0)),
        ],
        out_specs=[
            pl.BlockSpec((tm, tn), lambda i, j: (i, jnp.maximum(j - S5_TILES, 0))),
            pl.BlockSpec((tm, tn), lambda i, j: (i, jnp.minimum(j, S5_TILES - 1))),
            pl.BlockSpec((tm, LANES), lambda i, j: (i, 0)),
        ],
        out_shape=[
            jax.ShapeDtypeStruct((m, n - S5_WIDTH), BF16),
            jax.ShapeDtypeStruct((m, S5_WIDTH), F32),
            jax.ShapeDtypeStruct((m, LANES), F32),
        ],
        scratch_shapes=[pltpu.VMEM((tm, D_MODEL), BF16)],
        compiler_params=_cparams("parallel", "arbitrary"),
        name="inproj",
    )(h, g, w, wdt)


def _conv_chunk(ext_ref, x_f32, w_ref, b_ref, first):
    t = x_f32.shape[0]

    @pl.when(first)
    def _():
        ext_ref[0:CONV_PAD, :] = jnp.zeros((CONV_PAD, ext_ref.shape[1]), F32)

    ext_ref[CONV_PAD:CONV_PAD + t, :] = x_f32
    acc = b_ref[...] + w_ref[CONV_WIDTH - 1:CONV_WIDTH, :] * x_f32
    for k in range(CONV_WIDTH - 1):
        off = CONV_PAD - (CONV_WIDTH - 1) + k
        acc = acc + w_ref[k:k + 1, :] * ext_ref[off:off + t, :]
    ext_ref[CONV_PAD - (CONV_WIDTH - 1):CONV_PAD, :] = ext_ref[CONV_PAD + t - (CONV_WIDTH - 1):CONV_PAD + t, :]
    return acc


def _expand_heads(v, nslab):
    r = v.shape[0]
    lane = lax.broadcasted_iota(jnp.int32, (r, LANES), 1)
    slabs = []
    for k in range(nslab):
        lo = jnp.broadcast_to(v[:, 2 * k:2 * k + 1], (r, LANES))
        hi = jnp.broadcast_to(v[:, 2 * k + 1:2 * k + 2], (r, LANES))
        slabs.append(jnp.where(lane < SSD_HEAD_DIM, lo, hi))
    return jnp.concatenate(slabs, axis=1)


def _ssd_kernel(z_ref, xs_ref, bc_ref, dt_ref, wxs_ref, bxs_ref, wbc_ref, bbc_ref, dtb_ref, alog_ref,
                dskip_ref, ng_ref, o_ref, ext_xs, ext_bc, state_ref):
    q = SSD_CHUNK
    n = SSD_STATE
    gw = SSD_INNER // SSD_GROUPS
    first = pl.program_id(1) == 0

    @pl.when(first)
    def _():
        state_ref[...] = jnp.zeros(state_ref.shape, F32)

    xs = _silu(_conv_chunk(ext_xs, xs_ref[...].astype(F32), wxs_ref, bxs_ref, first))
    bc = _silu(_conv_chunk(ext_bc, bc_ref[...].astype(F32), wbc_ref, bbc_ref, first))

    dt = _softplus(dt_ref[...] + dtb_ref[...])
    a = -jnp.exp(alog_ref[...])
    da = dt * a
    row = lax.broadcasted_iota(jnp.int32, (q, q), 0)
    col = lax.broadcasted_iota(jnp.int32, (q, q), 1)
    causal = row >= col
    tril = causal.astype(F32)
    cs = jnp.dot(tril, da, preferred_element_type=F32, precision=lax.Precision.HIGHEST)
    cs_t = cs.T
    tot = cs[q - 1:q, :]

    xdt = xs * _expand_heads(dt, 8)
    xdt_b = xdt.astype(BF16)
    xdec_b = (xdt * _expand_heads(jnp.exp(tot - cs), 8)).astype(BF16)
    ecs = _expand_heads(jnp.exp(cs), 8)
    etot = _expand_heads(jnp.exp(tot), 8)
    lane = lax.broadcasted_iota(jnp.int32, (q, LANES), 1)

    y_groups = []
    for g in range(SSD_GROUPS):
        bg = bc[:, g * n:(g + 1) * n]
        cg = bc[:, SSD_GROUPS * n + g * n:SSD_GROUPS * n + (g + 1) * n].astype(BF16)
        cb = lax.dot_general(cg, bg.astype(BF16), (((1,), (1,)), ((), ())), preferred_element_type=F32)
        slabs = []
        for k in range(gw // LANES):
            res = []
            for hh in range(2):
                h = g * (SSD_HEADS // SSD_GROUPS) + 2 * k + hh
                seg = cs[:, h:h + 1] - cs_t[h:h + 1, :]
                lmat = jnp.exp(jnp.where(causal, seg, -jnp.inf))
                mh = (cb * lmat).astype(BF16)
                sl = g * gw + k * LANES
                res.append(jnp.dot(mh, xdt_b[:, sl:sl + LANES], preferred_element_type=F32))
            slabs.append(jnp.where(lane < SSD_HEAD_DIM, res[0], res[1]))
        y_diag = jnp.concatenate(slabs, axis=1)
        st = state_ref[g]
        y_off = jnp.dot(cg, st.astype(BF16), preferred_element_type=F32) * ecs[:, g * gw:(g + 1) * gw]
        upd = jnp.dot(bg.T.astype(BF16), xdec_b[:, g * gw:(g + 1) * gw], preferred_element_type=F32)
        state_ref[g] = st * etot[:, g * gw:(g + 1) * gw] + upd
        yg = y_diag + y_off + xs[:, g * gw:(g + 1) * gw] * dskip_ref[:, g * gw:(g + 1) * gw]
        yg = yg * _silu(z_ref[:, g * gw:(g + 1) * gw].astype(F32))
        yg = yg * lax.rsqrt(jnp.mean(yg * yg, axis=-1, keepdims=True) + EPS)
        y_groups.append(yg)
    y = jnp.concatenate(y_groups, axis=1) * ng_ref[...]
    o_ref[...] = y.astype(o_ref.dtype)


def _ssd(proj, dt_raw, wxs, bxs, wbc, bbc, dtb, alog, dskip, ng, batch, seq):
    q = SSD_CHUNK
    nc = seq // q
    m = batch * seq
    full = lambda shape: pl.BlockSpec(shape, lambda b, c: (0,) * len(shape))
    return pl.pallas_call(
        _ssd_kernel,
        grid=(batch, nc),
        in_specs=[
            pl.BlockSpec((q, 1024), lambda b, c: (b * nc + c, COL_Z)),
            pl.BlockSpec((q, 1024), lambda b, c: (b * nc + c, COL_XS)),
            pl.BlockSpec((q, 512), lambda b, c: (b * nc + c, COL_BC_512)),
            pl.BlockSpec((q, LANES), lambda b, c: (b * nc + c, 0)),
            full((CONV_WIDTH, 1024)), full((1, 1024)), full((CONV_WIDTH, 512)), full((1, 512)),
            full((1, LANES)), full((1, LANES)), full((1, 1024)), full((1, 1024)),
        ],
        out_specs=pl.BlockSpec((q, 1024), lambda b, c: (b * nc + c, 0)),
        out_shape=jax.ShapeDtypeStruct((m, SSD_INNER), BF16),
        scratch_shapes=[
            pltpu.VMEM((CONV_PAD + q, 1024), F32),
            pltpu.VMEM((CONV_PAD + q, 512), F32),
            pltpu.VMEM((SSD_GROUPS, SSD_STATE, SSD_INNER // SSD_GROUPS), F32),
        ],
        compiler_params=_cparams("parallel", "arbitrary"),
        name="ssd",
    )(proj, proj, proj, dt_raw, wxs, bxs, wbc, bbc, dtb, alog, dskip, ng)


def _lru_kernel(lg_ref, lx_ref, cw_ref, cb_ref, wbd_ref, ba_ref, bx_ref, lam_ref, o_ref, ext, carry_ref):
    t = lx_ref.shape[0]
    first = pl.program_id(1) == 0

    @pl.when(first)
    def _():
        carry_ref[...] = jnp.zeros(carry_ref.shape, F32)

    xc = _conv_chunk(ext, lx_ref[...].astype(F32), cw_ref, cb_ref, first)
    xcb = xc.astype(BF16)
    nblk = wbd_ref.shape[0]
    bw = LRU_WIDTH // nblk
    r_parts, i_parts = [], []
    for s in range(nblk):
        ga = jnp.dot(xcb[:, s * bw:(s + 1) * bw], wbd_ref[s], preferred_element_type=F32)
        r_parts.append(ga[:, :bw])
        i_parts.append(ga[:, bw:])
    r = jax.nn.sigmoid(jnp.concatenate(r_parts, axis=1) + ba_ref[...])
    i = jax.nn.sigmoid(jnp.concatenate(i_parts, axis=1) + bx_ref[...])
    log_a = (-LRU_C) * r * _softplus(-lam_ref[...])
    a = jnp.exp(log_a)
    u = jnp.sqrt(-jnp.tanh(log_a) * (a * a + 1.0)) * (i * xc)

    rowi = lax.broadcasted_iota(jnp.int32, (t, LRU_WIDTH), 0)
    sh = 1
    while sh < t:
        keep = rowi >= sh
        a_s = jnp.where(keep, pltpu.roll(a, sh, 0), 1.0)
        u_s = jnp.where(keep, pltpu.roll(u, sh, 0), 0.0)
        u = a * u_s + u
        a = a * a_s
        sh *= 2
    h = u + a * carry_ref[...]
    carry_ref[...] = h[t - 1:t, :]
    o_ref[...] = (h * jax.nn.gelu(lg_ref[...].astype(F32))).astype(o_ref.dtype)


def _lru(proj, cw, cb, wbd, ba, bx, lam, batch, seq, tc=256):
    nc = seq // tc
    m = batch * seq
    full = lambda shape: pl.BlockSpec(shape, lambda b, c: (0,) * len(shape))
    return pl.pallas_call(
        _lru_kernel,
        grid=(batch, nc),
        in_specs=[
            pl.BlockSpec((tc, 1024), lambda b, c: (b * nc + c, COL_LG)),
            pl.BlockSpec((tc, 1024), lambda b, c: (b * nc + c, COL_LX)),
            full((CONV_WIDTH, 1024)), full((1, 1024)), full(wbd.shape),
            full((1, 1024)), full((1, 1024)), full((1, 1024)),
        ],
        out_specs=pl.BlockSpec((tc, 1024), lambda b, c: (b * nc + c, 0)),
        out_shape=jax.ShapeDtypeStruct((m, LRU_WIDTH), BF16),
        scratch_shapes=[pltpu.VMEM((CONV_PAD + tc, 1024), F32), pltpu.VMEM((1, LRU_WIDTH), F32)],
        compiler_params=_cparams("parallel", "arbitrary"),
        name="rglru",
    )(proj, proj, cw, cb, wbd, ba, bx, lam)


def _s5_kernel(x_ref, wc_ref, wo_ref, ar_ref, ai_ref, d_ref, o_ref, *, nchunk):
    q = S5_CHUNK
    r = x_ref.shape[0] // q
    nst = S5_SLAB_GROUPS * S5_STATE
    xs = [x_ref[pl.ds(t, r, stride=q), :] for t in range(q)]
    acat = jnp.concatenate([x.astype(BF16) for x in xs], axis=1)
    res = jnp.dot(acat, wc_ref[0], preferred_element_type=F32)
    hs = res[:, q * LANES:]
    cidx = lax.broadcasted_iota(jnp.int32, (r, 2 * nst), 0) % nchunk
    ar = ar_ref[0]
    ai = ai_ref[0]
    j = 0
    sh = 1
    while sh < nchunk:
        prev = jnp.where(cidx >= sh, pltpu.roll(hs, sh, 0), 0.0)
        hs = hs + ar[j:j + 1, :] * prev + ai[j:j + 1, :] * pltpu.roll(prev, nst, 1)
        sh *= 2
        j += 1
    h_in = jnp.where(cidx >= 1, pltpu.roll(hs, 1, 0), 0.0)
    yoff = jnp.dot(h_in.astype(BF16), wo_ref[0], preferred_element_type=F32)
    d = d_ref[0]
    for t in range(q):
        y = res[:, t * LANES:(t + 1) * LANES] + yoff[:, t * LANES:(t + 1) * LANES] + xs[t] * d
        o_ref[pl.ds(t, r, stride=q), :] = jax.nn.gelu(y)


def _s5(u5, wc, wo, ar, ai, dflat, nchunk, rt=512):
    m = u5.shape[0]
    q = S5_CHUNK
    rt = min(rt, m // q)
    per_a = lambda a: pl.BlockSpec((1,) + a.shape[1:], lambda s, i: (s, 0, 0), pipeline_mode=SINGLE)
    return pl.pallas_call(
        functools.partial(_s5_kernel, nchunk=nchunk),
        grid=(S5_WIDTH // LANES, m // (q * rt)),
        in_specs=[pl.BlockSpec((q * rt, LANES), lambda s, i: (i, s)),
                  per_a(wc), per_a(wo), per_a(ar), per_a(ai), per_a(dflat)],
        out_specs=pl.BlockSpec((q * rt, LANES), lambda s, i: (i, s)),
        out_shape=jax.ShapeDtypeStruct((m, S5_WIDTH), F32),
        compiler_params=_cparams("parallel", "parallel"),
        name="s5",
    )(u5, wc, wo, ar, ai, dflat)


def _s5_matrices(lam_re, lam_im, b_re, b_im, c_re, c_im, d_skip, log_dt, nchunk):
    q = S5_CHUNK
    g_, p_, h_ = S5_GROUPS, S5_STATE, S5_GROUP
    ns, sg = S5_WIDTH // LANES, S5_SLAB_GROUPS
    dt = jnp.exp(log_dt)[:, None]
    lr, li = lam_re, lam_im

    def apow(k):
        kk = k[:, None, None].astype(F32)
        mag = jnp.exp(kk * (lr * dt)[None])
        ang = kk * (li * dt)[None]
        return mag * jnp.cos(ang), mag * jnp.sin(ang)

    abar_re, abar_im = apow(jnp.arange(1, 2))
    abar_re, abar_im = abar_re[0], abar_im[0]
    den = lr * lr + li * li
    nr, ni = abar_re - 1.0, abar_im
    q_re, q_im = (nr * lr + ni * li) / den, (ni * lr - nr * li) / den
    bb_re = q_re[..., None] * b_re - q_im[..., None] * b_im
    bb_im = q_re[..., None] * b_im + q_im[..., None] * b_re
    pw_re, pw_im = apow(jnp.arange(q + 1))
    ca_re = c_re[None] * pw_re[:, :, None, :] - c_im[None] * pw_im[:, :, None, :]
    ca_im = c_re[None] * pw_im[:, :, None, :] + c_im[None] * pw_re[:, :, None, :]
    kk = (jnp.einsum("kgip,gpj->kgij", ca_re[:q], bb_re, precision="highest")
          - jnp.einsum("kgip,gpj->kgij", ca_im[:q], bb_im, precision="highest"))
    s_idx = jnp.arange(q)[:, None]
    t_idx = jnp.arange(q)[None, :]
    lag = t_idx - s_idx
    kt = kk[jnp.clip(lag, 0, q - 1)]
    kt = jnp.where((lag >= 0)[:, :, None, None, None], kt, 0.0)
    eye = jnp.eye(sg, dtype=F32)
    w_toep = jnp.einsum("stagij,hg->ashjtgi", kt.reshape(q, q, ns, sg, h_, h_), eye)
    w_toep = w_toep.reshape(ns, q * LANES, q * LANES)
    rp_re, rp_im = pw_re[:q][::-1], pw_im[:q][::-1]
    bs_re = rp_re[..., None] * bb_re[None] - rp_im[..., None] * bb_im[None]
    bs_im = rp_re[..., None] * bb_im[None] + rp_im[..., None] * bb_re[None]
    bsc = jnp.stack([bs_re, bs_im]).reshape(2, q, ns, sg, p_, h_)
    w_state = jnp.einsum("rsagpj,hg->ashjrgp", bsc, eye).reshape(ns, q * LANES, 2 * sg * p_)
    wc = jnp.concatenate([w_toep, w_state], axis=2).astype(BF16)
    coc = jnp.stack([ca_re[1:], -ca_im[1:]]).reshape(2, q, ns, sg, h_, p_)
    wo = jnp.einsum("rtagip,hg->arhptgi", coc, eye).reshape(ns, 2 * sg * p_, q * LANES).astype(BF16)
    nstep = max(1, int(math.ceil(math.log2(nchunk)))) if nchunk > 1 else 1
    sp_re, sp_im = apow(q * (2 ** jnp.arange(nstep)))
    nrow = 8 * ((nstep + 7) // 8)
    sp_re = sp_re.reshape(nstep, ns, sg * p_).transpose(1, 0, 2)
    sp_im = sp_im.reshape(nstep, ns, sg * p_).transpose(1, 0, 2)
    pad = ((0, 0), (0, nrow - nstep), (0, 0))
    ar = jnp.pad(jnp.concatenate([sp_re, sp_re], axis=-1), pad)
    ai = jnp.pad(jnp.concatenate([-sp_im, sp_im], axis=-1), pad)
    dflat = d_skip.reshape(ns, 1, LANES).astype(F32)
    return wc, wo, ar, ai, dflat


def _merge_kernel(h_ref, ys_ref, yl_ref, y5_ref, g0_ref, g1_ref, g2_ref, ws_ref, wl_ref, w5_ref, wo_ref, o_ref):
    d = D_MODEL
    p_ssd = jnp.dot(ys_ref[...], ws_ref[...], preferred_element_type=F32)
    p_lru = jnp.dot(yl_ref[...], wl_ref[...], preferred_element_type=F32)
    p_s5 = jnp.dot(y5_ref[...].astype(BF16), w5_ref[...], preferred_element_type=F32)
    y_s5 = p_s5[:, :d] * jax.nn.sigmoid(p_s5[:, d:])
    merged = (jax.nn.sigmoid(g0_ref[...].astype(F32)) * p_ssd
              + jax.nn.sigmoid(g1_ref[...].astype(F32)) * p_lru
              + jax.nn.sigmoid(g2_ref[...].astype(F32)) * y_s5)
    o_ref[...] = h_ref[...] + jnp.dot(merged.astype(BF16), wo_ref[...], preferred_element_type=F32)


def _merge(h, y_ssd, y_lru, y_s5, proj, w_ssd, w_lru, w_s5, w_out, tm=512):
    m = h.shape[0]
    d = D_MODEL
    row = lambda c: pl.BlockSpec((tm, d), lambda i, c=c: (i, c))
    full = lambda shape: pl.BlockSpec(shape, lambda i: (0, 0), pipeline_mode=SINGLE)
    return pl.pallas_call(
        _merge_kernel,
        grid=(m // tm,),
        in_specs=[row(0), row(0), row(0), row(0), row(COL_GATE0), row(COL_GATE0 + 1), row(COL_GATE0 + 2),
                  full((d, d)), full((d, d)), full((d, 2 * d)), full((d, d))],
        out_specs=row(0),
        out_shape=jax.ShapeDtypeStruct((m, d), F32),
        compiler_params=_cparams("parallel"),
        name="merge",
    )(h, y_ssd, y_lru, y_s5, proj, proj, proj, w_ssd, w_lru, w_s5, w_out)


def _ffn_kernel(h_ref, g_ref, wg_ref, wu_ref, wd_ref, o_ref, hn_ref, acc_ref):
    f = pl.program_id(1)

    @pl.when(f == 0)
    def _():
        hn_ref[...] = _rms(h_ref[...], g_ref[...]).astype(BF16)
        acc_ref[...] = jnp.zeros(acc_ref.shape, F32)

    hn = hn_ref[...]
    a = jnp.dot(hn, wg_ref[...], preferred_element_type=F32)
    b = jnp.dot(hn, wu_ref[...], preferred_element_type=F32)
    acc_ref[...] += jnp.dot((_silu(a) * b).astype(BF16), wd_ref[...], preferred_element_type=F32)

    @pl.when(f == pl.num_programs(1) - 1)
    def _():
        o_ref[...] = h_ref[...] + acc_ref[...]


def _ffn(h, g, wg, wu, wd, tm=512, tf=1408):
    m = h.shape[0]
    d = D_MODEL
    nf = wg.shape[1] // tf
    return pl.pallas_call(
        _ffn_kernel,
        grid=(m // tm, nf),
        in_specs=[
            pl.BlockSpec((tm, d), lambda i, f: (i, 0)),
            pl.BlockSpec((1, d), lambda i, f: (0, 0)),
            pl.BlockSpec((d, tf), lambda i, f: (0, f)),
            pl.BlockSpec((d, tf), lambda i, f: (0, f)),
            pl.BlockSpec((tf, d), lambda i, f: (f, 0)),
        ],
        out_specs=pl.BlockSpec((tm, d), lambda i, f: (i, 0)),
        out_shape=jax.ShapeDtypeStruct((m, d), F32),
        scratch_shapes=[pltpu.VMEM((tm, d), BF16), pltpu.VMEM((tm, d), F32)],
        compiler_params=_cparams("parallel", "arbitrary"),
        name="ffn_dense",
    )(h, g, wg, wu, wd)


def _router_kernel(h_ref, g_ref, wr_ref, br_ref, o_ref):
    hn = _rms(h_ref[...], g_ref[...])
    logits = jnp.dot(hn, wr_ref[...], preferred_element_type=F32, precision=lax.Precision.HIGHEST) + br_ref[...]
    lane = lax.broadcasted_iota(jnp.int32, logits.shape, 1)
    logits = jnp.where(lane < N_EXPERTS, logits, -jnp.inf)
    m1 = jnp.max(logits, axis=-1, keepdims=True)
    i1 = jnp.min(jnp.where(logits == m1, lane, LANES), axis=-1, keepdims=True)
    rest = jnp.where(lane == i1, -jnp.inf, logits)
    m2 = jnp.max(rest, axis=-1, keepdims=True)
    i2 = jnp.min(jnp.where(rest == m2, lane, LANES), axis=-1, keepdims=True)
    e2 = jnp.exp(m2 - m1)
    w1 = 1.0 / (1.0 + e2)
    w2 = e2 / (1.0 + e2)
    o_ref[...] = jnp.where(lane == 0, i1.astype(F32),
                           jnp.where(lane == 1, i2.astype(F32),
                                     jnp.where(lane == 2, w1, jnp.where(lane == 3, w2, 0.0))))


def _router(h, g, wr, br, tm=512):
    m = h.shape[0]
    d = D_MODEL
    return pl.pallas_call(
        _router_kernel,
        grid=(m // tm,),
        in_specs=[
            pl.BlockSpec((tm, d), lambda i: (i, 0)),
            pl.BlockSpec((1, d), lambda i: (0, 0)),
            pl.BlockSpec((d, LANES), lambda i: (0, 0)),
            pl.BlockSpec((1, LANES), lambda i: (0, 0)),
        ],
        out_specs=pl.BlockSpec((tm, LANES), lambda i: (i, 0)),
        out_shape=jax.ShapeDtypeStruct((m, LANES), F32),
        compiler_params=_cparams("parallel"),
        name="router",
    )(h, g, wr, br)


def _moe_kernel(nblk_ref, h_ref, g_ref, ri_ref, wg_ref, wu_ref, wd_ref, fg_ref, o_ref,
                hn_ref, rank_ref, rankt_ref, xg_ref, acc_ref, *, final, cap):
    i = pl.program_id(0)
    e = pl.program_id(1)
    f = pl.program_id(2)
    ne = pl.num_programs(1)
    nf = pl.num_programs(2)
    tm = h_ref.shape[0]
    nb = nblk_ref[i * ne + e]
    ef = e.astype(F32)

    @pl.when((e == 0) & (f == 0))
    def _():
        h = h_ref[...]
        hn_ref[...] = _rms(h, g_ref[...]).astype(BF16)
        o_ref[...] = h
        lane = lax.broadcasted_iota(jnp.int32, (RANK_BLK, LANES), 1).astype(F32)
        rr = lax.broadcasted_iota(jnp.int32, (RANK_BLK, RANK_BLK), 0)
        cc = lax.broadcasted_iota(jnp.int32, (RANK_BLK, RANK_BLK), 1)
        tril = jnp.where(rr >= cc, 1.0, 0.0).astype(BF16)
        carry = jnp.zeros((1, LANES), F32)
        for s in range(tm // RANK_BLK):
            ri = ri_ref[s * RANK_BLK:(s + 1) * RANK_BLK, :]
            mask = jnp.where((lane == ri[:, 0:1]) | (lane == ri[:, 1:2]), 1.0, 0.0)
            cnt = jnp.dot(tril, mask.astype(BF16), preferred_element_type=F32) + carry
            rank_ref[s * RANK_BLK:(s + 1) * RANK_BLK, :] = cnt * mask
            carry = cnt[RANK_BLK - 1:RANK_BLK, :]
        rankt_ref[...] = rank_ref[...].T

    def row_base(b):
        return pl.multiple_of(b * cap, 16)

    @pl.when(f == 0)
    def _():
        rt = rankt_ref[pl.ds(e, 1), :]
        hn = hn_ref[...]

        def gather_blk(b, c):
            rid = (lax.broadcasted_iota(jnp.int32, (cap, tm), 0) + (b * cap + 1)).astype(F32)
            sel = jnp.where(rt == rid, 1.0, 0.0).astype(BF16)
            xg_ref[pl.ds(row_base(b), cap), :] = jnp.dot(sel, hn, preferred_element_type=F32).astype(BF16)
            acc_ref[pl.ds(row_base(b), cap), :] = jnp.zeros((cap, D_MODEL), F32)
            return c

        lax.fori_loop(0, nb, gather_blk, 0)

    def ffn_blk(b, c):
        x = xg_ref[pl.ds(row_base(b), cap), :]
        a = jnp.dot(x, wg_ref[0], preferred_element_type=F32)
        u = jnp.dot(x, wu_ref[0], preferred_element_type=F32)
        acc_ref[pl.ds(row_base(b), cap), :] += jnp.dot((_silu(a) * u).astype(BF16), wd_ref[0],
                                                       preferred_element_type=F32)
        return c

    lax.fori_loop(0, nb, ffn_blk, 0)

    @pl.when(f == nf - 1)
    def _():
        ri = ri_ref[...]
        we = jnp.where(ri[:, 0:1] == ef, ri[:, 2:3], 0.0) + jnp.where(ri[:, 1:2] == ef, ri[:, 3:4], 0.0)
        lane = lax.broadcasted_iota(jnp.int32, (tm, LANES), 1)
        rk = jnp.sum(jnp.where(lane == e, rank_ref[...], 0.0), axis=-1, keepdims=True)

        def scatter_blk(b, c):
            cid = (lax.broadcasted_iota(jnp.int32, (tm, cap), 1) + (b * cap + 1)).astype(F32)
            sel_t = jnp.where(rk == cid, 1.0, 0.0).astype(BF16)
            y = jnp.dot(sel_t, acc_ref[pl.ds(row_base(b), cap), :].astype(BF16), preferred_element_type=F32)
            o_ref[...] += we * y
            return c

        lax.fori_loop(0, nb, scatter_blk, 0)

        if final:
            @pl.when(e == ne - 1)
            def _():
                o_ref[...] = _rms(o_ref[...], fg_ref[...])


def _moe(h, g, rinfo, nblk, wg, wu, wd, fg, final, tf=896):
    m = h.shape[0]
    d = D_MODEL
    tm = min(MOE_TM, m)
    cap = MOE_CAP
    ne, _, fe = wg.shape
    max_rows = -(-tm // cap) * cap
    grid_spec = pltpu.PrefetchScalarGridSpec(
        num_scalar_prefetch=1,
        grid=(m // tm, ne, fe // tf),
        in_specs=[
            pl.BlockSpec((tm, d), lambda i, e, f, nb: (i, 0)),
            pl.BlockSpec((1, d), lambda i, e, f, nb: (0, 0)),
            pl.BlockSpec((tm, LANES), lambda i, e, f, nb: (i, 0)),
            pl.BlockSpec((1, d, tf), lambda i, e, f, nb: (e, 0, f)),
            pl.BlockSpec((1, d, tf), lambda i, e, f, nb: (e, 0, f)),
            pl.BlockSpec((1, tf, d), lambda i, e, f, nb: (e, f, 0)),
            pl.BlockSpec((1, d), lambda i, e, f, nb: (0, 0)),
        ],
        out_specs=pl.BlockSpec((tm, d), lambda i, e, f, nb: (i, 0)),
        scratch_shapes=[
            pltpu.VMEM((tm, d), BF16),
            pltpu.VMEM((tm, LANES), F32),
            pltpu.VMEM((LANES, tm), F32),
            pltpu.VMEM((max_rows, d), BF16),
            pltpu.VMEM((max_rows, d), F32),
        ],
    )
    return pl.pallas_call(
        functools.partial(_moe_kernel, final=final, cap=cap),
        grid_spec=grid_spec,
        out_shape=jax.ShapeDtypeStruct((m, d), F32),
        compiler_params=_cparams("parallel", "arbitrary", "arbitrary"),
        name="moe_experts",
    )(nblk, h, g, rinfo, wg, wu, wd, fg)


def _moe_block_counts(rinfo, tm, cap):
    m = rinfo.shape[0]
    ids = rinfo[:, 0:2].astype(jnp.int32)
    onehot = (ids[:, :, None] == jnp.arange(N_EXPERTS, dtype=jnp.int32)).any(axis=1)
    counts = onehot.reshape(m // tm, tm, N_EXPERTS).astype(jnp.int32).sum(axis=1)
    return ((counts + cap - 1) // cap).reshape(-1)


def _final_norm_kernel(h_ref, g_ref, o_ref):
    o_ref[...] = _rms(h_ref[...], g_ref[...])


def _final_norm(h, g, tm=1024):
    m, d = h.shape
    return pl.pallas_call(
        _final_norm_kernel,
        grid=(m // tm,),
        in_specs=[pl.BlockSpec((tm, d), lambda i: (i, 0)), pl.BlockSpec((1, d), lambda i: (0, 0))],
        out_specs=pl.BlockSpec((tm, d), lambda i: (i, 0)),
        out_shape=jax.ShapeDtypeStruct((m, d), F32),
        compiler_params=_cparams("parallel"),
        name="final_norm",
    )(h, g)


def _pack_w_in(w):
    z = w[:, 0:1024]
    xs = w[:, 1024:2048]
    bc = w[:, 2048:2560]
    dt = w[:, 2560:2576]
    lg = w[:, 2576:3600]
    lx = w[:, 3600:4624]
    s5 = w[:, 4624:5648]
    gates = w[:, 5648:8720]
    packed = jnp.concatenate([s5, z, xs, lg, lx, gates, bc], axis=1).astype(BF16)
    wdt = jnp.pad(dt, ((0, 0), (0, LANES - SSD_HEADS))).astype(BF16)
    return packed, wdt


def _block_diag_gates(w_a, w_x, bw=256):
    per = bw // LRU_HEAD_DIM
    nblk = LRU_WIDTH // bw
    eye = jnp.eye(per, dtype=F32)

    def bd(w):
        w = w.reshape(nblk, per, LRU_HEAD_DIM, LRU_HEAD_DIM)
        return jnp.einsum("sbij,bc->sbicj", w, eye).reshape(nblk, bw, bw)

    return jnp.concatenate([bd(w_a), bd(w_x)], axis=2).astype(BF16)


def _row(v, width=None):
    v = v.reshape(1, -1).astype(F32)
    if width is not None and v.shape[1] < width:
        v = jnp.pad(v, ((0, 0), (0, width - v.shape[1])))
    return v


def kernel(x, mix_norm, w_in, ssd_conv_w, ssd_conv_b, ssd_dt_bias, ssd_a_log, ssd_d, ssd_norm, w_branch_ssd, lru_conv_w, lru_conv_b, lru_w_a, lru_b_a, lru_w_x, lru_b_x, lru_lambda, w_branch_lru, s5_lam_re, s5_lam_im, s5_b_re, s5_b_im, s5_c_re, s5_c_im, s5_d, s5_log_dt, w_branch_s5, w_out, ffn_norm, dense_w_gate, dense_w_up, dense_w_down, moe_router, moe_router_bias, moe_w_gate, moe_w_up, moe_w_down, final_norm):
    batch, seq, d = x.shape
    depth = mix_norm.shape[0]
    m = batch * seq
    nchunk5 = seq // S5_CHUNK
    h = x.reshape(m, d)
    normed = False
    for layer in range(depth):
        w_pack, w_dt = _pack_w_in(w_in[layer])
        proj, u5, dt_raw = _inproj(h, _row(mix_norm[layer]), w_pack, w_dt)

        cw = ssd_conv_w[layer]
        cb = ssd_conv_b[layer]
        alog = jnp.pad(ssd_a_log[layer].reshape(1, -1), ((0, 0), (0, LANES - SSD_HEADS)), constant_values=-jnp.inf)
        y_ssd = _ssd(proj, dt_raw, cw[:, :SSD_INNER], _row(cb[:SSD_INNER]), cw[:, SSD_INNER:], _row(cb[SSD_INNER:]),
                     _row(ssd_dt_bias[layer], LANES), alog, _row(jnp.repeat(ssd_d[layer], SSD_HEAD_DIM)),
                     _row(ssd_norm[layer]), batch, seq)

        y_lru = _lru(proj, lru_conv_w[layer], _row(lru_conv_b[layer]),
                     _block_diag_gates(lru_w_a[layer], lru_w_x[layer]), _row(lru_b_a[layer]),
                     _row(lru_b_x[layer]), _row(lru_lambda[layer]), batch, seq)

        wc5, wo5, ar, ai, dflat = _s5_matrices(
            s5_lam_re[layer], s5_lam_im[layer], s5_b_re[layer], s5_b_im[layer], s5_c_re[layer], s5_c_im[layer],
            s5_d[layer], s5_log_dt[layer], nchunk5)
        y_s5 = _s5(u5, wc5, wo5, ar, ai, dflat, nchunk5)

        h = _merge(h, y_ssd, y_lru, y_s5, proj, w_branch_ssd[layer].astype(BF16), w_branch_lru[layer].astype(BF16),
                   w_branch_s5[layer].astype(BF16), w_out[layer].astype(BF16))

        j = layer // 2
        if layer % 2 == 0:
            h = _ffn(h, _row(ffn_norm[layer]), dense_w_gate[j].astype(BF16), dense_w_up[j].astype(BF16),
                     dense_w_down[j].astype(BF16))
        else:
            gn = _row(ffn_norm[layer])
            wr = jnp.pad(moe_router[j], ((0, 0), (0, LANES - N_EXPERTS)))
            rinfo = _router(h, gn, wr, _row(moe_router_bias[j], LANES))
            nblk = _moe_block_counts(rinfo, min(MOE_TM, m), MOE_CAP)
            normed = layer == depth - 1
            h = _moe(h, gn, rinfo, nblk, moe_w_gate[j].astype(BF16), moe_w_up[j].astype(BF16),
                     moe_w_down[j].astype(BF16), _row(final_norm), normed)
    if not normed:
        h = _final_norm(h, _row(final_norm))
    return h.reshape(batch, seq, d)
```

```python
import functools
import math

import jax
import jax.numpy as jnp
from jax import lax
from jax.experimental import pallas as pl
from jax.experimental.pallas import tpu as pltpu

F32 = jnp.float32
BF16 = jnp.bfloat16

D_MODEL = 1024
SSD_HEADS = 16
SSD_HEAD_DIM = 64
SSD_INNER = 1024
SSD_GROUPS = 2
SSD_STATE = 128
SSD_CHUNK = 256
CONV_WIDTH = 4
LRU_WIDTH = 1024
LRU_HEADS = 16
LRU_HEAD_DIM = 64
LRU_C = 8.0
S5_WIDTH = 1024
S5_GROUP = 16
S5_GROUPS = 64
S5_STATE = 64
S5_CHUNK = 16
S5_SLAB_GROUPS = 8
N_EXPERTS = 8
EPS = 1e-6

LANES = 128
CONV_PAD = 8

COL_Z, COL_XS, COL_LG, COL_LX, COL_GATE0 = 0, 1, 2, 3, 4
COL_BC_512 = 14
INPROJ_TN = 512
S5_TILES = S5_WIDTH // INPROJ_TN

MOE_TM = 2048
MOE_CAP = 576
MOE_SEL_CHUNK = 512
MOE_VMEM_LIMIT = 58 * 1024 * 1024
RANK_BLK = 256

VMEM_LIMIT = 52 * 1024 * 1024
SINGLE = pl.Buffered(1)


def _cparams(*sem):
    return pltpu.CompilerParams(dimension_semantics=sem, vmem_limit_bytes=VMEM_LIMIT)


def _rms(x, g):
    return x * lax.rsqrt(jnp.mean(x * x, axis=-1, keepdims=True) + EPS) * g


def _softplus(x):
    return jnp.maximum(x, 0.0) + jnp.log1p(jnp.exp(-jnp.abs(x)))


def _silu(x):
    return x * jax.nn.sigmoid(x)


def _inproj_kernel(x_ref, g_ref, w_ref, wdt_ref, o_ref, u5_ref, dt_ref, xn_ref):
    j = pl.program_id(1)

    @pl.when(j == 0)
    def _():
        xn = _rms(x_ref[...], g_ref[...]).astype(BF16)
        xn_ref[...] = xn
        dt_ref[...] = jnp.dot(xn, wdt_ref[...], preferred_element_type=F32)

    res = jnp.dot(xn_ref[...], w_ref[...], preferred_element_type=F32)

    @pl.when(j < S5_TILES)
    def _():
        u5_ref[...] = res

    @pl.when(j >= S5_TILES)
    def _():
        o_ref[...] = res.astype(o_ref.dtype)


def _inproj(h, g, w, wdt, tm=2048):
    m = h.shape[0]
    tm = min(tm, m)
    tn = INPROJ_TN
    n = w.shape[1]
    return pl.pallas_call(
        _inproj_kernel,
        grid=(m // tm, n // tn),
        in_specs=[
            pl.BlockSpec((tm, D_MODEL), lambda i, j: (i, 0)),
            pl.BlockSpec((1, D_MODEL), lambda i, j: (0, 0)),
            pl.BlockSpec((D_MODEL, tn), lambda i, j: (0, j)),
            pl.BlockSpec((D_MODEL, LANES), lambda i, j: (0, 0)),
        ],
        out_specs=[
            pl.BlockSpec((tm, tn), lambda i, j: (i, jnp.maximum(j - S5_TILES, 0))),
            pl.BlockSpec((tm, tn), lambda i, j: (i, jnp.minimum(j, S5_TILES - 1))),
            pl.BlockSpec((tm, LANES), lambda i, j: (i, 0)),
        ],
        out_shape=[
            jax.ShapeDtypeStruct((m, n - S5_WIDTH), BF16),
            jax.ShapeDtypeStruct((m, S5_WIDTH), F32),
            jax.ShapeDtypeStruct((m, LANES), F32),
        ],
        scratch_shapes=[pltpu.VMEM((tm, D_MODEL), BF16)],
        compiler_params=_cparams("parallel", "arbitrary"),
        name="inproj",
    )(h, g, w, wdt)


def _conv_chunk(ext_ref, x_f32, w_ref, b_ref, first):
    t = x_f32.shape[0]

    @pl.when(first)
    def _():
        ext_ref[0:CONV_PAD, :] = jnp.zeros((CONV_PAD, ext_ref.shape[1]), F32)

    ext_ref[CONV_PAD:CONV_PAD + t, :] = x_f32
    acc = b_ref[...] + w_ref[CONV_WIDTH - 1:CONV_WIDTH, :] * x_f32
    for k in range(CONV_WIDTH - 1):
        off = CONV_PAD - (CONV_WIDTH - 1) + k
        acc = acc + w_ref[k:k + 1, :] * ext_ref[off:off + t, :]
    ext_ref[CONV_PAD - (CONV_WIDTH - 1):CONV_PAD, :] = ext_ref[CONV_PAD + t - (CONV_WIDTH - 1):CONV_PAD + t, :]
    return acc


def _expand_heads(v, nslab):
    r = v.shape[0]
    lane = lax.broadcasted_iota(jnp.int32, (r, LANES), 1)
    slabs = []
    for k in range(nslab):
        lo = jnp.broadcast_to(v[:, 2 * k:2 * k + 1], (r, LANES))
        hi = jnp.broadcast_to(v[:, 2 * k + 1:2 * k + 2], (r, LANES))
        slabs.append(jnp.where(lane < SSD_HEAD_DIM, lo, hi))
    return jnp.concatenate(slabs, axis=1)


def _ssd_kernel(z_ref, xs_ref, bc_ref, dt_ref, wxs_ref, bxs_ref, wbc_ref, bbc_ref, dtb_ref, alog_ref,
                dskip_ref, ng_ref, o_ref, ext_xs, ext_bc, state_ref):
    q = SSD_CHUNK
    n = SSD_STATE
    gw = SSD_INNER // SSD_GROUPS
    first = pl.program_id(1) == 0

    @pl.when(first)
    def _():
        state_ref[...] = jnp.zeros(state_ref.shape, F32)

    xs = _silu(_conv_chunk(ext_xs, xs_ref[...].astype(F32), wxs_ref, bxs_ref, first))
    bc = _silu(_conv_chunk(ext_bc, bc_ref[...].astype(F32), wbc_ref, bbc_ref, first))

    dt = _softplus(dt_ref[...] + dtb_ref[...])
    a = -jnp.exp(alog_ref[...])
    da = dt * a
    row = lax.broadcasted_iota(jnp.int32, (q, q), 0)
    col = lax.broadcasted_iota(jnp.int32, (q, q), 1)
    causal = row >= col
    tril = causal.astype(F32)
    cs = jnp.dot(tril, da, preferred_element_type=F32, precision=lax.Precision.HIGHEST)
    cs_t = cs.T
    tot = cs[q - 1:q, :]

    xdt = xs * _expand_heads(dt, 8)
    xdt_b = xdt.astype(BF16)
    xdec_b = (xdt * _expand_heads(jnp.exp(tot - cs), 8)).astype(BF16)
    ecs = _expand_heads(jnp.exp(cs), 8)
    etot = _expand_heads(jnp.exp(tot), 8)
    lane = lax.broadcasted_iota(jnp.int32, (q, LANES), 1)

    y_groups = []
    for g in range(SSD_GROUPS):
        bg = bc[:, g * n:(g + 1) * n]
        cg = bc[:, SSD_GROUPS * n + g * n:SSD_GROUPS * n + (g + 1) * n].astype(BF16)
        cb = lax.dot_general(cg, bg.astype(BF16), (((1,), (1,)), ((), ())), preferred_element_type=F32)
        slabs = []
        for k in range(gw // LANES):
            res = []
            for hh in range(2):
                h = g * (SSD_HEADS // SSD_GROUPS) + 2 * k + hh
                seg = cs[:, h:h + 1] - cs_t[h:h + 1, :]
                lmat = jnp.exp(jnp.where(causal, seg, -jnp.inf))
                mh = (cb * lmat).astype(BF16)
                sl = g * gw + k * LANES
                res.append(jnp.dot(mh, xdt_b[:, sl:sl + LANES], preferred_element_type=F32))
            slabs.append(jnp.where(lane < SSD_HEAD_DIM, res[0], res[1]))
        y_diag = jnp.concatenate(slabs, axis=1)
        st = state_ref[g]
        y_off = jnp.dot(cg, st.astype(BF16), preferred_element_type=F32) * ecs[:, g * gw:(g + 1) * gw]
        upd = jnp.dot(bg.T.astype(BF16), xdec_b[:, g * gw:(g + 1) * gw], preferred_element_type=F32)
        state_ref[g] = st * etot[:, g * gw:(g + 1) * gw] + upd
        yg = y_diag + y_off + xs[:, g * gw:(g + 1) * gw] * dskip_ref[:, g * gw:(g + 1) * gw]
        yg = yg * _silu(z_ref[:, g * gw:(g + 1) * gw].astype(F32))
        yg = yg * lax.rsqrt(jnp.mean(yg * yg, axis=-1, keepdims=True) + EPS)
        y_groups.append(yg)
    y = jnp.concatenate(y_groups, axis=1) * ng_ref[...]
    o_ref[...] = y.astype(o_ref.dtype)


def _ssd(proj, dt_raw, wxs, bxs, wbc, bbc, dtb, alog, dskip, ng, batch, seq):
    q = SSD_CHUNK
    nc = seq // q
    m = batch * seq
    full = lambda shape: pl.BlockSpec(shape, lambda b, c: (0,) * len(shape))
    return pl.pallas_call(
        _ssd_kernel,
        grid=(batch, nc),
        in_specs=[
            pl.BlockSpec((q, 1024), lambda b, c: (b * nc + c, COL_Z)),
            pl.BlockSpec((q, 1024), lambda b, c: (b * nc + c, COL_XS)),
            pl.BlockSpec((q, 512), lambda b, c: (b * nc + c, COL_BC_512)),
            pl.BlockSpec((q, LANES), lambda b, c: (b * nc + c, 0)),
            full((CONV_WIDTH, 1024)), full((1, 1024)), full((CONV_WIDTH, 512)), full((1, 512)),
            full((1, LANES)), full((1, LANES)), full((1, 1024)), full((1, 1024)),
        ],
        out_specs=pl.BlockSpec((q, 1024), lambda b, c: (b * nc + c, 0)),
        out_shape=jax.ShapeDtypeStruct((m, SSD_INNER), BF16),
        scratch_shapes=[
            pltpu.VMEM((CONV_PAD + q, 1024), F32),
            pltpu.VMEM((CONV_PAD + q, 512), F32),
            pltpu.VMEM((SSD_GROUPS, SSD_STATE, SSD_INNER // SSD_GROUPS), F32),
        ],
        compiler_params=_cparams("parallel", "arbitrary"),
        name="ssd",
    )(proj, proj, proj, dt_raw, wxs, bxs, wbc, bbc, dtb, alog, dskip, ng)


def _lru_kernel(lg_ref, lx_ref, cw_ref, cb_ref, wbd_ref, ba_ref, bx_ref, lam_ref, o_ref, ext, carry_ref):
    t = lx_ref.shape[0]
    first = pl.program_id(1) == 0

    @pl.when(first)
    def _():
        carry_ref[...] = jnp.zeros(carry_ref.shape, F32)

    xc = _conv_chunk(ext, lx_ref[...].astype(F32), cw_ref, cb_ref, first)
    xcb = xc.astype(BF16)
    nblk = wbd_ref.shape[0]
    bw = LRU_WIDTH // nblk
    r_parts, i_parts = [], []
    for s in range(nblk):
        ga = jnp.dot(xcb[:, s * bw:(s + 1) * bw], wbd_ref[s], preferred_element_type=F32)
        r_parts.append(ga[:, :bw])
        i_parts.append(ga[:, bw:])
    r = jax.nn.sigmoid(jnp.concatenate(r_parts, axis=1) + ba_ref[...])
    i = jax.nn.sigmoid(jnp.concatenate(i_parts, axis=1) + bx_ref[...])
    log_a = (-LRU_C) * r * _softplus(-lam_ref[...])
    a = jnp.exp(log_a)
    u = jnp.sqrt(-jnp.tanh(log_a) * (a * a + 1.0)) * (i * xc)

    rowi = lax.broadcasted_iota(jnp.int32, (t, LRU_WIDTH), 0)
    sh = 1
    while sh < t:
        keep = rowi >= sh
        a_s = jnp.where(keep, pltpu.roll(a, sh, 0), 1.0)
        u_s = jnp.where(keep, pltpu.roll(u, sh, 0), 0.0)
        u = a * u_s + u
        a = a * a_s
        sh *= 2
    h = u + a * carry_ref[...]
    carry_ref[...] = h[t - 1:t, :]
    o_ref[...] = (h * jax.nn.gelu(lg_ref[...].astype(F32))).astype(o_ref.dtype)


def _lru(proj, cw, cb, wbd, ba, bx, lam, batch, seq, tc=256):
    nc = seq // tc
    m = batch * seq
    full = lambda shape: pl.BlockSpec(shape, lambda b, c: (0,) * len(shape))
    return pl.pallas_call(
        _lru_kernel,
        grid=(batch, nc),
        in_specs=[
            pl.BlockSpec((tc, 1024), lambda b, c: (b * nc + c, COL_LG)),
            pl.BlockSpec((tc, 1024), lambda b, c: (b * nc + c, COL_LX)),
            full((CONV_WIDTH, 1024)), full((1, 1024)), full(wbd.shape),
            full((1, 1024)), full((1, 1024)), full((1, 1024)),
        ],
        out_specs=pl.BlockSpec((tc, 1024), lambda b, c: (b * nc + c, 0)),
        out_shape=jax.ShapeDtypeStruct((m, LRU_WIDTH), BF16),
        scratch_shapes=[pltpu.VMEM((CONV_PAD + tc, 1024), F32), pltpu.VMEM((1, LRU_WIDTH), F32)],
        compiler_params=_cparams("parallel", "arbitrary"),
        name="rglru",
    )(proj, proj, cw, cb, wbd, ba, bx, lam)


def _s5_kernel(x_ref, wc_ref, wo_ref, ar_ref, ai_ref, d_ref, o_ref, *, nchunk):
    q = S5_CHUNK
    r = x_ref.shape[0] // q
    nst = S5_SLAB_GROUPS * S5_STATE
    xs = [x_ref[pl.ds(t, r, stride=q), :] for t in range(q)]
    acat = jnp.concatenate([x.astype(BF16) for x in xs], axis=1)
    res = jnp.dot(acat, wc_ref[0], preferred_element_type=F32)
    hs = res[:, q * LANES:]
    cidx = lax.broadcasted_iota(jnp.int32, (r, 2 * nst), 0) % nchunk
    ar = ar_ref[0]
    ai = ai_ref[0]
    j = 0
    sh = 1
    while sh < nchunk:
        prev = jnp.where(cidx >= sh, pltpu.roll(hs, sh, 0), 0.0)
        hs = hs + ar[j:j + 1, :] * prev + ai[j:j + 1, :] * pltpu.roll(prev, nst, 1)
        sh *= 2
        j += 1
    h_in = jnp.where(cidx >= 1, pltpu.roll(hs, 1, 0), 0.0)
    yoff = jnp.dot(h_in.astype(BF16), wo_ref[0], preferred_element_type=F32)
    d = d_ref[0]
    for t in range(q):
        y = res[:, t * LANES:(t + 1) * LANES] + yoff[:, t * LANES:(t + 1) * LANES] + xs[t] * d
        o_ref[pl.ds(t, r, stride=q), :] = jax.nn.gelu(y)


def _s5(u5, wc, wo, ar, ai, dflat, nchunk, rt=512):
    m = u5.shape[0]
    q = S5_CHUNK
    rt = min(rt, m // q)
    per_a = lambda a: pl.BlockSpec((1,) + a.shape[1:], lambda s, i: (s, 0, 0), pipeline_mode=SINGLE)
    return pl.pallas_call(
        functools.partial(_s5_kernel, nchunk=nchunk),
        grid=(S5_WIDTH // LANES, m // (q * rt)),
        in_specs=[pl.BlockSpec((q * rt, LANES), lambda s, i: (i, s)),
                  per_a(wc), per_a(wo), per_a(ar), per_a(ai), per_a(dflat)],
        out_specs=pl.BlockSpec((q * rt, LANES), lambda s, i: (i, s)),
        out_shape=jax.ShapeDtypeStruct((m, S5_WIDTH), F32),
        compiler_params=_cparams("parallel", "parallel"),
        name="s5",
    )(u5, wc, wo, ar, ai, dflat)


def _s5_matrices(lam_re, lam_im, b_re, b_im, c_re, c_im, d_skip, log_dt, nchunk):
    q = S5_CHUNK
    g_, p_, h_ = S5_GROUPS, S5_STATE, S5_GROUP
    ns, sg = S5_WIDTH // LANES, S5_SLAB_GROUPS
    dt = jnp.exp(log_dt)[:, None]
    lr, li = lam_re, lam_im

    def apow(k):
        kk = k[:, None, None].astype(F32)
        mag = jnp.exp(kk * (lr * dt)[None])
        ang = kk * (li * dt)[None]
        return mag * jnp.cos(ang), mag * jnp.sin(ang)

    abar_re, abar_im = apow(jnp.arange(1, 2))
    abar_re, abar_im = abar_re[0], abar_im[0]
    den = lr * lr + li * li
    nr, ni = abar_re - 1.0, abar_im
    q_re, q_im = (nr * lr + ni * li) / den, (ni * lr - nr * li) / den
    bb_re = q_re[..., None] * b_re - q_im[..., None] * b_im
    bb_im = q_re[..., None] * b_im + q_im[..., None] * b_re
    pw_re, pw_im = apow(jnp.arange(q + 1))
    ca_re = c_re[None] * pw_re[:, :, None, :] - c_im[None] * pw_im[:, :, None, :]
    ca_im = c_re[None] * pw_im[:, :, None, :] + c_im[None] * pw_re[:, :, None, :]
    kk = (jnp.einsum("kgip,gpj->kgij", ca_re[:q], bb_re, precision="highest")
          - jnp.einsum("kgip,gpj->kgij", ca_im[:q], bb_im, precision="highest"))
    s_idx = jnp.arange(q)[:, None]
    t_idx = jnp.arange(q)[None, :]
    lag = t_idx - s_idx
    same_g = jnp.eye(sg, dtype=bool)
    kk_t = kk.reshape(q, ns, sg, h_, h_).transpose(0, 1, 4, 2, 3)
    bd = jnp.where(same_g[None, None, :, None, :, None], kk_t[:, :, None], 0.0)
    bd = bd.reshape(q, ns, LANES, LANES).astype(BF16)
    blocks = jnp.where((lag >= 0)[:, :, None, None, None], bd[jnp.clip(lag, 0, q - 1)], 0)
    w_toep = blocks.transpose(2, 0, 3, 1, 4).reshape(ns, q * LANES, q * LANES)
    rp_re, rp_im = pw_re[:q][::-1], pw_im[:q][::-1]
    bs_re = rp_re[..., None] * bb_re[None] - rp_im[..., None] * bb_im[None]
    bs_im = rp_re[..., None] * bb_im[None] + rp_im[..., None] * bb_re[None]
    bsc = jnp.stack([bs_re, bs_im]).reshape(2, q, ns, sg, p_, h_)
    bsc = bsc.transpose(2, 1, 5, 0, 3, 4).astype(BF16)
    w_state = jnp.where(same_g[None, None, :, None, None, :, None], bsc[:, :, None], 0)
    w_state = w_state.reshape(ns, q * LANES, 2 * sg * p_)
    wc = jnp.concatenate([w_toep, w_state], axis=2)
    coc = jnp.stack([ca_re[1:], -ca_im[1:]]).reshape(2, q, ns, sg, h_, p_)
    coc = coc.transpose(2, 0, 5, 1, 3, 4).astype(BF16)
    wo = jnp.where(same_g[None, None, :, None, None, :, None], coc[:, :, None], 0)
    wo = wo.reshape(ns, 2 * sg * p_, q * LANES)
    nstep = max(1, int(math.ceil(math.log2(nchunk)))) if nchunk > 1 else 1
    sp_re, sp_im = apow(q * (2 ** jnp.arange(nstep)))
    nrow = 8 * ((nstep + 7) // 8)
    sp_re = sp_re.reshape(nstep, ns, sg * p_).transpose(1, 0, 2)
    sp_im = sp_im.reshape(nstep, ns, sg * p_).transpose(1, 0, 2)
    pad = ((0, 0), (0, nrow - nstep), (0, 0))
    ar = jnp.pad(jnp.concatenate([sp_re, sp_re], axis=-1), pad)
    ai = jnp.pad(jnp.concatenate([-sp_im, sp_im], axis=-1), pad)
    dflat = d_skip.reshape(ns, 1, LANES).astype(F32)
    return wc, wo, ar, ai, dflat


def _merge_kernel(h_ref, ys_ref, yl_ref, y5_ref, g0_ref, g1_ref, g2_ref, ws_ref, wl_ref, w5_ref, wo_ref, o_ref):
    d = D_MODEL
    p_ssd = jnp.dot(ys_ref[...], ws_ref[...], preferred_element_type=F32)
    p_lru = jnp.dot(yl_ref[...], wl_ref[...], preferred_element_type=F32)
    p_s5 = jnp.dot(y5_ref[...].astype(BF16), w5_ref[...], preferred_element_type=F32)
    y_s5 = p_s5[:, :d] * jax.nn.sigmoid(p_s5[:, d:])
    merged = (jax.nn.sigmoid(g0_ref[...].astype(F32)) * p_ssd
              + jax.nn.sigmoid(g1_ref[...].astype(F32)) * p_lru
              + jax.nn.sigmoid(g2_ref[...].astype(F32)) * y_s5)
    o_ref[...] = h_ref[...] + jnp.dot(merged.astype(BF16), wo_ref[...], preferred_element_type=F32)


def _merge(h, y_ssd, y_lru, y_s5, proj, w_ssd, w_lru, w_s5, w_out, tm=512):
    m = h.shape[0]
    d = D_MODEL
    row = lambda c: pl.BlockSpec((tm, d), lambda i, c=c: (i, c))
    full = lambda shape: pl.BlockSpec(shape, lambda i: (0, 0), pipeline_mode=SINGLE)
    return pl.pallas_call(
        _merge_kernel,
        grid=(m // tm,),
        in_specs=[row(0), row(0), row(0), row(0), row(COL_GATE0), row(COL_GATE0 + 1), row(COL_GATE0 + 2),
                  full((d, d)), full((d, d)), full((d, 2 * d)), full((d, d))],
        out_specs=row(0),
        out_shape=jax.ShapeDtypeStruct((m, d), F32),
        compiler_params=_cparams("parallel"),
        name="merge",
    )(h, y_ssd, y_lru, y_s5, proj, proj, proj, w_ssd, w_lru, w_s5, w_out)


def _ffn_kernel(h_ref, g_ref, wg_ref, wu_ref, wd_ref, o_ref, hn_ref, acc_ref):
    f = pl.program_id(1)

    @pl.when(f == 0)
    def _():
        hn_ref[...] = _rms(h_ref[...], g_ref[...]).astype(BF16)
        acc_ref[...] = jnp.zeros(acc_ref.shape, F32)

    hn = hn_ref[...]
    a = jnp.dot(hn, wg_ref[...], preferred_element_type=F32)
    b = jnp.dot(hn, wu_ref[...], preferred_element_type=F32)
    acc_ref[...] += jnp.dot((_silu(a) * b).astype(BF16), wd_ref[...], preferred_element_type=F32)

    @pl.when(f == pl.num_programs(1) - 1)
    def _():
        o_ref[...] = h_ref[...] + acc_ref[...]


def _ffn(h, g, wg, wu, wd, tm=512, tf=1408):
    m = h.shape[0]
    d = D_MODEL
    nf = wg.shape[1] // tf
    return pl.pallas_call(
        _ffn_kernel,
        grid=(m // tm, nf),
        in_specs=[
            pl.BlockSpec((tm, d), lambda i, f: (i, 0)),
            pl.BlockSpec((1, d), lambda i, f: (0, 0)),
            pl.BlockSpec((d, tf), lambda i, f: (0, f)),
            pl.BlockSpec((d, tf), lambda i, f: (0, f)),
            pl.BlockSpec((tf, d), lambda i, f: (f, 0)),
        ],
        out_specs=pl.BlockSpec((tm, d), lambda i, f: (i, 0)),
        out_shape=jax.ShapeDtypeStruct((m, d), F32),
        scratch_shapes=[pltpu.VMEM((tm, d), BF16), pltpu.VMEM((tm, d), F32)],
        compiler_params=_cparams("parallel", "arbitrary"),
        name="ffn_dense",
    )(h, g, wg, wu, wd)


def _router_kernel(h_ref, g_ref, wr_ref, br_ref, o_ref):
    hn = _rms(h_ref[...], g_ref[...])
    logits = jnp.dot(hn, wr_ref[...], preferred_element_type=F32, precision=lax.Precision.HIGHEST) + br_ref[...]
    lane = lax.broadcasted_iota(jnp.int32, logits.shape, 1)
    logits = jnp.where(lane < N_EXPERTS, logits, -jnp.inf)
    m1 = jnp.max(logits, axis=-1, keepdims=True)
    i1 = jnp.min(jnp.where(logits == m1, lane, LANES), axis=-1, keepdims=True)
    rest = jnp.where(lane == i1, -jnp.inf, logits)
    m2 = jnp.max(rest, axis=-1, keepdims=True)
    i2 = jnp.min(jnp.where(rest == m2, lane, LANES), axis=-1, keepdims=True)
    e2 = jnp.exp(m2 - m1)
    w1 = 1.0 / (1.0 + e2)
    w2 = e2 / (1.0 + e2)
    o_ref[...] = jnp.where(lane == 0, i1.astype(F32),
                           jnp.where(lane == 1, i2.astype(F32),
                                     jnp.where(lane == 2, w1, jnp.where(lane == 3, w2, 0.0))))


def _router(h, g, wr, br, tm=512):
    m = h.shape[0]
    d = D_MODEL
    return pl.pallas_call(
        _router_kernel,
        grid=(m // tm,),
        in_specs=[
            pl.BlockSpec((tm, d), lambda i: (i, 0)),
            pl.BlockSpec((1, d), lambda i: (0, 0)),
            pl.BlockSpec((d, LANES), lambda i: (0, 0)),
            pl.BlockSpec((1, LANES), lambda i: (0, 0)),
        ],
        out_specs=pl.BlockSpec((tm, LANES), lambda i: (i, 0)),
        out_shape=jax.ShapeDtypeStruct((m, LANES), F32),
        compiler_params=_cparams("parallel"),
        name="router",
    )(h, g, wr, br)


def _moe_kernel(nblk_ref, h_ref, g_ref, ri_ref, wg_ref, wu_ref, wd_ref, fg_ref, o_ref,
                hn_ref, rank_ref, rankt_ref, xg_ref, acc_ref, *, final, cap):
    i = pl.program_id(0)
    e = pl.program_id(1)
    f = pl.program_id(2)
    ne = pl.num_programs(1)
    nf = pl.num_programs(2)
    tm = h_ref.shape[0]
    nb = nblk_ref[i * ne + e]
    ef = e.astype(F32)

    @pl.when((e == 0) & (f == 0))
    def _():
        h = h_ref[...]
        hn_ref[...] = _rms(h, g_ref[...]).astype(BF16)
        o_ref[...] = h
        lane = lax.broadcasted_iota(jnp.int32, (RANK_BLK, LANES), 1).astype(F32)
        rr = lax.broadcasted_iota(jnp.int32, (RANK_BLK, RANK_BLK), 0)
        cc = lax.broadcasted_iota(jnp.int32, (RANK_BLK, RANK_BLK), 1)
        tril = jnp.where(rr >= cc, 1.0, 0.0).astype(BF16)
        carry = jnp.zeros((1, LANES), F32)
        for s in range(tm // RANK_BLK):
            ri = ri_ref[s * RANK_BLK:(s + 1) * RANK_BLK, :]
            mask = jnp.where((lane == ri[:, 0:1]) | (lane == ri[:, 1:2]), 1.0, 0.0)
            cnt = jnp.dot(tril, mask.astype(BF16), preferred_element_type=F32) + carry
            rank_ref[s * RANK_BLK:(s + 1) * RANK_BLK, :] = cnt * mask
            carry = cnt[RANK_BLK - 1:RANK_BLK, :]
        rankt_ref[...] = rank_ref[...].T

    def row_base(b):
        return pl.multiple_of(b * cap, 16)

    @pl.when(f == 0)
    def _():
        rt = rankt_ref[pl.ds(e, 1), :]
        hn = hn_ref[...]

        def gather_blk(b, c):
            xg = jnp.zeros((cap, D_MODEL), F32)
            for k0 in range(0, tm, MOE_SEL_CHUNK):
                rid = (lax.broadcasted_iota(jnp.int32, (cap, MOE_SEL_CHUNK), 0) + (b * cap + 1)).astype(F32)
                sel = jnp.where(rt[:, k0:k0 + MOE_SEL_CHUNK] == rid, 1.0, 0.0).astype(BF16)
                xg = xg + jnp.dot(sel, hn[k0:k0 + MOE_SEL_CHUNK, :], preferred_element_type=F32)
            xg_ref[pl.ds(row_base(b), cap), :] = xg.astype(BF16)
            acc_ref[pl.ds(row_base(b), cap), :] = jnp.zeros((cap, D_MODEL), F32)
            return c

        lax.fori_loop(0, nb, gather_blk, 0)

    def ffn_blk(b, c):
        x = xg_ref[pl.ds(row_base(b), cap), :]
        a = jnp.dot(x, wg_ref[0], preferred_element_type=F32)
        u = jnp.dot(x, wu_ref[0], preferred_element_type=F32)
        acc_ref[pl.ds(row_base(b), cap), :] += jnp.dot((_silu(a) * u).astype(BF16), wd_ref[0],
                                                       preferred_element_type=F32)
        return c

    lax.fori_loop(0, nb, ffn_blk, 0)

    @pl.when(f == nf - 1)
    def _():
        ri = ri_ref[...]
        we = jnp.where(ri[:, 0:1] == ef, ri[:, 2:3], 0.0) + jnp.where(ri[:, 1:2] == ef, ri[:, 3:4], 0.0)
        lane = lax.broadcasted_iota(jnp.int32, (tm, LANES), 1)
        rk = jnp.sum(jnp.where(lane == e, rank_ref[...], 0.0), axis=-1, keepdims=True)

        def scatter_blk(b, c):
            yb = acc_ref[pl.ds(row_base(b), cap), :].astype(BF16)
            for r0 in range(0, tm, MOE_SEL_CHUNK):
                cid = (lax.broadcasted_iota(jnp.int32, (MOE_SEL_CHUNK, cap), 1) + (b * cap + 1)).astype(F32)
                sel_t = jnp.where(rk[r0:r0 + MOE_SEL_CHUNK, :] == cid, 1.0, 0.0).astype(BF16)
                y = jnp.dot(sel_t, yb, preferred_element_type=F32)
                o_ref[r0:r0 + MOE_SEL_CHUNK, :] += we[r0:r0 + MOE_SEL_CHUNK, :] * y
            return c

        lax.fori_loop(0, nb, scatter_blk, 0)

        if final:
            @pl.when(e == ne - 1)
            def _():
                o_ref[...] = _rms(o_ref[...], fg_ref[...])


def _moe(h, g, rinfo, nblk, wg, wu, wd, fg, final, tf=512):
    m = h.shape[0]
    d = D_MODEL
    tm = min(MOE_TM, m)
    cap = MOE_CAP
    ne, _, fe = wg.shape
    max_rows = -(-tm // cap) * cap
    grid_spec = pltpu.PrefetchScalarGridSpec(
        num_scalar_prefetch=1,
        grid=(m // tm, ne, fe // tf),
        in_specs=[
            pl.BlockSpec((tm, d), lambda i, e, f, nb: (i, 0), pipeline_mode=SINGLE),
            pl.BlockSpec((1, d), lambda i, e, f, nb: (0, 0)),
            pl.BlockSpec((tm, LANES), lambda i, e, f, nb: (i, 0), pipeline_mode=SINGLE),
            pl.BlockSpec((1, d, tf), lambda i, e, f, nb: (e, 0, f)),
            pl.BlockSpec((1, d, tf), lambda i, e, f, nb: (e, 0, f)),
            pl.BlockSpec((1, tf, d), lambda i, e, f, nb: (e, f, 0)),
            pl.BlockSpec((1, d), lambda i, e, f, nb: (0, 0)),
        ],
        out_specs=pl.BlockSpec((tm, d), lambda i, e, f, nb: (i, 0), pipeline_mode=SINGLE),
        scratch_shapes=[
            pltpu.VMEM((tm, d), BF16),
            pltpu.VMEM((tm, LANES), F32),
            pltpu.VMEM((LANES, tm), F32),
            pltpu.VMEM((max_rows, d), BF16),
            pltpu.VMEM((max_rows, d), F32),
        ],
    )
    return pl.pallas_call(
        functools.partial(_moe_kernel, final=final, cap=cap),
        grid_spec=grid_spec,
        out_shape=jax.ShapeDtypeStruct((m, d), F32),
        compiler_params=pltpu.CompilerParams(dimension_semantics=("parallel", "arbitrary", "arbitrary"),
                                             vmem_limit_bytes=MOE_VMEM_LIMIT),
        name="moe_experts",
    )(nblk, h, g, rinfo, wg, wu, wd, fg)


def _moe_block_counts(rinfo, tm, cap):
    m = rinfo.shape[0]
    ids = rinfo[:, 0:2].astype(jnp.int32)
    onehot = (ids[:, :, None] == jnp.arange(N_EXPERTS, dtype=jnp.int32)).any(axis=1)
    counts = onehot.reshape(m // tm, tm, N_EXPERTS).astype(jnp.int32).sum(axis=1)
    return ((counts + cap - 1) // cap).reshape(-1)


def _final_norm_kernel(h_ref, g_ref, o_ref):
    o_ref[...] = _rms(h_ref[...], g_ref[...])


def _final_norm(h, g, tm=1024):
    m, d = h.shape
    return pl.pallas_call(
        _final_norm_kernel,
        grid=(m // tm,),
        in_specs=[pl.BlockSpec((tm, d), lambda i: (i, 0)), pl.BlockSpec((1, d), lambda i: (0, 0))],
        out_specs=pl.BlockSpec((tm, d), lambda i: (i, 0)),
        out_shape=jax.ShapeDtypeStruct((m, d), F32),
        compiler_params=_cparams("parallel"),
        name="final_norm",
    )(h, g)


def _pack_w_in(w):
    z = w[:, 0:1024]
    xs = w[:, 1024:2048]
    bc = w[:, 2048:2560]
    dt = w[:, 2560:2576]
    lg = w[:, 2576:3600]
    lx = w[:, 3600:4624]
    s5 = w[:, 4624:5648]
    gates = w[:, 5648:8720]
    packed = jnp.concatenate([s5, z, xs, lg, lx, gates, bc], axis=1).astype(BF16)
    wdt = jnp.pad(dt, ((0, 0), (0, LANES - SSD_HEADS))).astype(BF16)
    return packed, wdt


def _block_diag_gates(w_a, w_x, bw=256):
    per = bw // LRU_HEAD_DIM
    nblk = LRU_WIDTH // bw
    eye = jnp.eye(per, dtype=F32)

    def bd(w):
        w = w.reshape(nblk, per, LRU_HEAD_DIM, LRU_HEAD_DIM)
        return jnp.einsum("sbij,bc->sbicj", w, eye).reshape(nblk, bw, bw)

    return jnp.concatenate([bd(w_a), bd(w_x)], axis=2).astype(BF16)


def _row(v, width=None):
    v = v.reshape(1, -1).astype(F32)
    if width is not None and v.shape[1] < width:
        v = jnp.pad(v, ((0, 0), (0, width - v.shape[1])))
    return v


def kernel(x, mix_norm, w_in, ssd_conv_w, ssd_conv_b, ssd_dt_bias, ssd_a_log, ssd_d, ssd_norm, w_branch_ssd, lru_conv_w, lru_conv_b, lru_w_a, lru_b_a, lru_w_x, lru_b_x, lru_lambda, w_branch_lru, s5_lam_re, s5_lam_im, s5_b_re, s5_b_im, s5_c_re, s5_c_im, s5_d, s5_log_dt, w_branch_s5, w_out, ffn_norm, dense_w_gate, dense_w_up, dense_w_down, moe_router, moe_router_bias, moe_w_gate, moe_w_up, moe_w_down, final_norm):
    batch, seq, d = x.shape
    depth = mix_norm.shape[0]
    m = batch * seq
    nchunk5 = seq // S5_CHUNK
    h = x.reshape(m, d)
    normed = False
    for layer in range(depth):
        w_pack, w_dt = _pack_w_in(w_in[layer])
        proj, u5, dt_raw = _inproj(h, _row(mix_norm[layer]), w_pack, w_dt)

        cw = ssd_conv_w[layer]
        cb = ssd_conv_b[layer]
        alog = jnp.pad(ssd_a_log[layer].reshape(1, -1), ((0, 0), (0, LANES - SSD_HEADS)), constant_values=-jnp.inf)
        y_ssd = _ssd(proj, dt_raw, cw[:, :SSD_INNER], _row(cb[:SSD_INNER]), cw[:, SSD_INNER:], _row(cb[SSD_INNER:]),
                     _row(ssd_dt_bias[layer], LANES), alog, _row(jnp.repeat(ssd_d[layer], SSD_HEAD_DIM)),
                     _row(ssd_norm[layer]), batch, seq)

        y_lru = _lru(proj, lru_conv_w[layer], _row(lru_conv_b[layer]),
                     _block_diag_gates(lru_w_a[layer], lru_w_x[layer]), _row(lru_b_a[layer]),
                     _row(lru_b_x[layer]), _row(lru_lambda[layer]), batch, seq)

        wc5, wo5, ar, ai, dflat = _s5_matrices(
            s5_lam_re[layer], s5_lam_im[layer], s5_b_re[layer], s5_b_im[layer], s5_c_re[layer], s5_c_im[layer],
            s5_d[layer], s5_log_dt[layer], nchunk5)
        y_s5 = _s5(u5, wc5, wo5, ar, ai, dflat, nchunk5)

        h = _merge(h, y_ssd, y_lru, y_s5, proj, w_branch_ssd[layer].astype(BF16), w_branch_lru[layer].astype(BF16),
                   w_branch_s5[layer].astype(BF16), w_out[layer].astype(BF16))

        j = layer // 2
        if layer % 2 == 0:
            h = _ffn(h, _row(ffn_norm[layer]), dense_w_gate[j].astype(BF16), dense_w_up[j].astype(BF16),
                     dense_w_down[j].astype(BF16))
        else:
            gn = _row(ffn_norm[layer])
            wr = jnp.pad(moe_router[j], ((0, 0), (0, LANES - N_EXPERTS)))
            rinfo = _router(h, gn, wr, _row(moe_router_bias[j], LANES))
            nblk = _moe_block_counts(rinfo, min(MOE_TM, m), MOE_CAP)
            normed = layer == depth - 1
            h = _moe(h, gn, rinfo, nblk, moe_w_gate[j].astype(BF16), moe_w_up[j].astype(BF16),
                     moe_w_down[j].astype(BF16), _row(final_norm), normed)
    if not normed:
        h = _final_norm(h, _row(final_norm))
    return h.reshape(batch, seq, d)
```

```python
import functools
import math

import jax
import jax.numpy as jnp
from jax import lax
from jax.experimental import pallas as pl
from jax.experimental.pallas import tpu as pltpu

F32 = jnp.float32
BF16 = jnp.bfloat16

D_MODEL = 1024
SSD_HEADS = 16
SSD_HEAD_DIM = 64
SSD_INNER = 1024
SSD_GROUPS = 2
SSD_STATE = 128
SSD_CHUNK = 256
CONV_WIDTH = 4
LRU_WIDTH = 1024
LRU_HEADS = 16
LRU_HEAD_DIM = 64
LRU_C = 8.0
S5_WIDTH = 1024
S5_GROUP = 16
S5_GROUPS = 64
S5_STATE = 64
S5_CHUNK = 16
S5_SLAB_GROUPS = 8
N_EXPERTS = 8
EPS = 1e-6

LANES = 128
CONV_PAD = 8

COL_Z, COL_XS, COL_LG, COL_LX, COL_GATE0 = 0, 1, 2, 3, 4
COL_BC_512 = 14
INPROJ_TN = 512
S5_TILES = S5_WIDTH // INPROJ_TN

MOE_TM = 1024
MOE_CAP = 304
MOE_SEL_CHUNK = 512
MOE_VMEM_LIMIT = 58 * 1024 * 1024
RANK_BLK = 256

VMEM_LIMIT = 52 * 1024 * 1024
SINGLE = pl.Buffered(1)


def _cparams(*sem):
    return pltpu.CompilerParams(dimension_semantics=sem, vmem_limit_bytes=VMEM_LIMIT)


def _rms(x, g):
    return x * lax.rsqrt(jnp.mean(x * x, axis=-1, keepdims=True) + EPS) * g


def _softplus(x):
    return jnp.maximum(x, 0.0) + jnp.log1p(jnp.exp(-jnp.abs(x)))


def _silu(x):
    return x * jax.nn.sigmoid(x)


def _inproj_kernel(x_ref, g_ref, w_ref, wdt_ref, o_ref, u5_ref, dt_ref, xn_ref):
    j = pl.program_id(1)

    @pl.when(j == 0)
    def _():
        xn = _rms(x_ref[...], g_ref[...]).astype(BF16)
        xn_ref[...] = xn
        dt_ref[...] = jnp.dot(xn, wdt_ref[...], preferred_element_type=F32)

    res = jnp.dot(xn_ref[...], w_ref[...], preferred_element_type=F32)

    @pl.when(j < S5_TILES)
    def _():
        u5_ref[...] = res

    @pl.when(j >= S5_TILES)
    def _():
        o_ref[...] = res.astype(o_ref.dtype)


def _inproj(h, g, w, wdt, tm=2048):
    m = h.shape[0]
    tm = min(tm, m)
    tn = INPROJ_TN
    n = w.shape[1]
    return pl.pallas_call(
        _inproj_kernel,
        grid=(m // tm, n // tn),
        in_specs=[
            pl.BlockSpec((tm, D_MODEL), lambda i, j: (i, 0)),
            pl.BlockSpec((1, D_MODEL), lambda i, j: (0, 0)),
            pl.BlockSpec((D_MODEL, tn), lambda i, j: (0, j)),
            pl.BlockSpec((D_MODEL, LANES), lambda i, j: (0, 0)),
        ],
        out_specs=[
            pl.BlockSpec((tm, tn), lambda i, j: (i, jnp.maximum(j - S5_TILES, 0))),
            pl.BlockSpec((tm, tn), lambda i, j: (i, jnp.minimum(j, S5_TILES - 1))),
            pl.BlockSpec((tm, LANES), lambda i, j: (i, 0)),
        ],
        out_shape=[
            jax.ShapeDtypeStruct((m, n - S5_WIDTH), BF16),
            jax.ShapeDtypeStruct((m, S5_WIDTH), F32),
            jax.ShapeDtypeStruct((m, LANES), F32),
        ],
        scratch_shapes=[pltpu.VMEM((tm, D_MODEL), BF16)],
        compiler_params=_cparams("parallel", "arbitrary"),
        name="inproj",
    )(h, g, w, wdt)


def _conv_chunk(ext_ref, x_f32, w_ref, b_ref, first):
    t = x_f32.shape[0]

    @pl.when(first)
    def _():
        ext_ref[0:CONV_PAD, :] = jnp.zeros((CONV_PAD, ext_ref.shape[1]), F32)

    ext_ref[CONV_PAD:CONV_PAD + t, :] = x_f32
    acc = b_ref[...] + w_ref[CONV_WIDTH - 1:CONV_WIDTH, :] * x_f32
    for k in range(CONV_WIDTH - 1):
        off = CONV_PAD - (CONV_WIDTH - 1) + k
        acc = acc + w_ref[k:k + 1, :] * ext_ref[off:off + t, :]
    ext_ref[CONV_PAD - (CONV_WIDTH - 1):CONV_PAD, :] = ext_ref[CONV_PAD + t - (CONV_WIDTH - 1):CONV_PAD + t, :]
    return acc


def _expand_heads(v, nslab):
    r = v.shape[0]
    lane = lax.broadcasted_iota(jnp.int32, (r, LANES), 1)
    slabs = []
    for k in range(nslab):
        lo = jnp.broadcast_to(v[:, 2 * k:2 * k + 1], (r, LANES))
        hi = jnp.broadcast_to(v[:, 2 * k + 1:2 * k + 2], (r, LANES))
        slabs.append(jnp.where(lane < SSD_HEAD_DIM, lo, hi))
    return jnp.concatenate(slabs, axis=1)


def _ssd_kernel(z_ref, xs_ref, bc_ref, dt_ref, wxs_ref, bxs_ref, wbc_ref, bbc_ref, dtb_ref, alog_ref,
                dskip_ref, ng_ref, o_ref, ext_xs, ext_bc, state_ref):
    q = SSD_CHUNK
    n = SSD_STATE
    gw = SSD_INNER // SSD_GROUPS
    first = pl.program_id(1) == 0

    @pl.when(first)
    def _():
        state_ref[...] = jnp.zeros(state_ref.shape, F32)

    xs = _silu(_conv_chunk(ext_xs, xs_ref[...].astype(F32), wxs_ref, bxs_ref, first))
    bc = _silu(_conv_chunk(ext_bc, bc_ref[...].astype(F32), wbc_ref, bbc_ref, first))

    dt = _softplus(dt_ref[...] + dtb_ref[...])
    a = -jnp.exp(alog_ref[...])
    da = dt * a
    row = lax.broadcasted_iota(jnp.int32, (q, q), 0)
    col = lax.broadcasted_iota(jnp.int32, (q, q), 1)
    causal = row >= col
    tril = causal.astype(F32)
    cs = jnp.dot(tril, da, preferred_element_type=F32, precision=lax.Precision.HIGHEST)
    cs_t = cs.T
    tot = cs[q - 1:q, :]

    xdt = xs * _expand_heads(dt, 8)
    xdt_b = xdt.astype(BF16)
    xdec_b = (xdt * _expand_heads(jnp.exp(tot - cs), 8)).astype(BF16)
    ecs = _expand_heads(jnp.exp(cs), 8)
    etot = _expand_heads(jnp.exp(tot), 8)
    lane = lax.broadcasted_iota(jnp.int32, (q, LANES), 1)

    y_groups = []
    for g in range(SSD_GROUPS):
        bg = bc[:, g * n:(g + 1) * n]
        cg = bc[:, SSD_GROUPS * n + g * n:SSD_GROUPS * n + (g + 1) * n].astype(BF16)
        cb = lax.dot_general(cg, bg.astype(BF16), (((1,), (1,)), ((), ())), preferred_element_type=F32)
        slabs = []
        for k in range(gw // LANES):
            res = []
            for hh in range(2):
                h = g * (SSD_HEADS // SSD_GROUPS) + 2 * k + hh
                seg = cs[:, h:h + 1] - cs_t[h:h + 1, :]
                lmat = jnp.exp(jnp.where(causal, seg, -jnp.inf))
                mh = (cb * lmat).astype(BF16)
                sl = g * gw + k * LANES
                res.append(jnp.dot(mh, xdt_b[:, sl:sl + LANES], preferred_element_type=F32))
            slabs.append(jnp.where(lane < SSD_HEAD_DIM, res[0], res[1]))
        y_diag = jnp.concatenate(slabs, axis=1)
        st = state_ref[g]
        y_off = jnp.dot(cg, st.astype(BF16), preferred_element_type=F32) * ecs[:, g * gw:(g + 1) * gw]
        upd = jnp.dot(bg.T.astype(BF16), xdec_b[:, g * gw:(g + 1) * gw], preferred_element_type=F32)
        state_ref[g] = st * etot[:, g * gw:(g + 1) * gw] + upd
        yg = y_diag + y_off + xs[:, g * gw:(g + 1) * gw] * dskip_ref[:, g * gw:(g + 1) * gw]
        yg = yg * _silu(z_ref[:, g * gw:(g + 1) * gw].astype(F32))
        yg = yg * lax.rsqrt(jnp.mean(yg * yg, axis=-1, keepdims=True) + EPS)
        y_groups.append(yg)
    y = jnp.concatenate(y_groups, axis=1) * ng_ref[...]
    o_ref[...] = y.astype(o_ref.dtype)


def _ssd(proj, dt_raw, wxs, bxs, wbc, bbc, dtb, alog, dskip, ng, batch, seq):
    q = SSD_CHUNK
    nc = seq // q
    m = batch * seq
    full = lambda shape: pl.BlockSpec(shape, lambda b, c: (0,) * len(shape))
    return pl.pallas_call(
        _ssd_kernel,
        grid=(batch, nc),
        in_specs=[
            pl.BlockSpec((q, 1024), lambda b, c: (b * nc + c, COL_Z)),
            pl.BlockSpec((q, 1024), lambda b, c: (b * nc + c, COL_XS)),
            pl.BlockSpec((q, 512), lambda b, c: (b * nc + c, COL_BC_512)),
            pl.BlockSpec((q, LANES), lambda b, c: (b * nc + c, 0)),
            full((CONV_WIDTH, 1024)), full((1, 1024)), full((CONV_WIDTH, 512)), full((1, 512)),
            full((1, LANES)), full((1, LANES)), full((1, 1024)), full((1, 1024)),
        ],
        out_specs=pl.BlockSpec((q, 1024), lambda b, c: (b * nc + c, 0)),
        out_shape=jax.ShapeDtypeStruct((m, SSD_INNER), BF16),
        scratch_shapes=[
            pltpu.VMEM((CONV_PAD + q, 1024), F32),
            pltpu.VMEM((CONV_PAD + q, 512), F32),
            pltpu.VMEM((SSD_GROUPS, SSD_STATE, SSD_INNER // SSD_GROUPS), F32),
        ],
        compiler_params=_cparams("parallel", "arbitrary"),
        name="ssd",
    )(proj, proj, proj, dt_raw, wxs, bxs, wbc, bbc, dtb, alog, dskip, ng)


def _lru_kernel(lg_ref, lx_ref, cw_ref, cb_ref, wbd_ref, ba_ref, bx_ref, lam_ref, o_ref, ext, carry_ref):
    t = lx_ref.shape[0]
    first = pl.program_id(1) == 0

    @pl.when(first)
    def _():
        carry_ref[...] = jnp.zeros(carry_ref.shape, F32)

    xc = _conv_chunk(ext, lx_ref[...].astype(F32), cw_ref, cb_ref, first)
    xcb = xc.astype(BF16)
    nblk = wbd_ref.shape[0]
    bw = LRU_WIDTH // nblk
    r_parts, i_parts = [], []
    for s in range(nblk):
        ga = jnp.dot(xcb[:, s * bw:(s + 1) * bw], wbd_ref[s], preferred_element_type=F32)
        r_parts.append(ga[:, :bw])
        i_parts.append(ga[:, bw:])
    r = jax.nn.sigmoid(jnp.concatenate(r_parts, axis=1) + ba_ref[...])
    i = jax.nn.sigmoid(jnp.concatenate(i_parts, axis=1) + bx_ref[...])
    log_a = (-LRU_C) * r * _softplus(-lam_ref[...])
    a = jnp.exp(log_a)
    u = jnp.sqrt(-jnp.tanh(log_a) * (a * a + 1.0)) * (i * xc)

    rowi = lax.broadcasted_iota(jnp.int32, (t, LRU_WIDTH), 0)
    sh = 1
    while sh < t:
        keep = rowi >= sh
        a_s = jnp.where(keep, pltpu.roll(a, sh, 0), 1.0)
        u_s = jnp.where(keep, pltpu.roll(u, sh, 0), 0.0)
        u = a * u_s + u
        a = a * a_s
        sh *= 2
    h = u + a * carry_ref[...]
    carry_ref[...] = h[t - 1:t, :]
    o_ref[...] = (h * jax.nn.gelu(lg_ref[...].astype(F32))).astype(o_ref.dtype)


def _lru(proj, cw, cb, wbd, ba, bx, lam, batch, seq, tc=256):
    nc = seq // tc
    m = batch * seq
    full = lambda shape: pl.BlockSpec(shape, lambda b, c: (0,) * len(shape))
    return pl.pallas_call(
        _lru_kernel,
        grid=(batch, nc),
        in_specs=[
            pl.BlockSpec((tc, 1024), lambda b, c: (b * nc + c, COL_LG)),
            pl.BlockSpec((tc, 1024), lambda b, c: (b * nc + c, COL_LX)),
            full((CONV_WIDTH, 1024)), full((1, 1024)), full(wbd.shape),
            full((1, 1024)), full((1, 1024)), full((1, 1024)),
        ],
        out_specs=pl.BlockSpec((tc, 1024), lambda b, c: (b * nc + c, 0)),
        out_shape=jax.ShapeDtypeStruct((m, LRU_WIDTH), BF16),
        scratch_shapes=[pltpu.VMEM((CONV_PAD + tc, 1024), F32), pltpu.VMEM((1, LRU_WIDTH), F32)],
        compiler_params=_cparams("parallel", "arbitrary"),
        name="rglru",
    )(proj, proj, cw, cb, wbd, ba, bx, lam)


def _s5_kernel(x_ref, wc_ref, wo_ref, ar_ref, ai_ref, d_ref, o_ref, *, nchunk):
    q = S5_CHUNK
    r = x_ref.shape[0] // q
    nst = S5_SLAB_GROUPS * S5_STATE
    xs = [x_ref[pl.ds(t, r, stride=q), :] for t in range(q)]
    acat = jnp.concatenate([x.astype(BF16) for x in xs], axis=1)
    res = jnp.dot(acat, wc_ref[0], preferred_element_type=F32)
    hs = res[:, q * LANES:]
    cidx = lax.broadcasted_iota(jnp.int32, (r, 2 * nst), 0) % nchunk
    ar = ar_ref[0]
    ai = ai_ref[0]
    j = 0
    sh = 1
    while sh < nchunk:
        prev = jnp.where(cidx >= sh, pltpu.roll(hs, sh, 0), 0.0)
        hs = hs + ar[j:j + 1, :] * prev + ai[j:j + 1, :] * pltpu.roll(prev, nst, 1)
        sh *= 2
        j += 1
    h_in = jnp.where(cidx >= 1, pltpu.roll(hs, 1, 0), 0.0)
    yoff = jnp.dot(h_in.astype(BF16), wo_ref[0], preferred_element_type=F32)
    d = d_ref[0]
    for t in range(q):
        y = res[:, t * LANES:(t + 1) * LANES] + yoff[:, t * LANES:(t + 1) * LANES] + xs[t] * d
        o_ref[pl.ds(t, r, stride=q), :] = jax.nn.gelu(y)


def _s5(u5, wc, wo, ar, ai, dflat, nchunk, rt=512):
    m = u5.shape[0]
    q = S5_CHUNK
    rt = min(rt, m // q)
    per_a = lambda a: pl.BlockSpec((1,) + a.shape[1:], lambda s, i: (s, 0, 0), pipeline_mode=SINGLE)
    return pl.pallas_call(
        functools.partial(_s5_kernel, nchunk=nchunk),
        grid=(S5_WIDTH // LANES, m // (q * rt)),
        in_specs=[pl.BlockSpec((q * rt, LANES), lambda s, i: (i, s)),
                  per_a(wc), per_a(wo), per_a(ar), per_a(ai), per_a(dflat)],
        out_specs=pl.BlockSpec((q * rt, LANES), lambda s, i: (i, s)),
        out_shape=jax.ShapeDtypeStruct((m, S5_WIDTH), F32),
        compiler_params=_cparams("parallel", "parallel"),
        name="s5",
    )(u5, wc, wo, ar, ai, dflat)


def _s5_matrices(lam_re, lam_im, b_re, b_im, c_re, c_im, d_skip, log_dt, nchunk):
    q = S5_CHUNK
    g_, p_, h_ = S5_GROUPS, S5_STATE, S5_GROUP
    ns, sg = S5_WIDTH // LANES, S5_SLAB_GROUPS
    dt = jnp.exp(log_dt)[:, None]
    lr, li = lam_re, lam_im

    def apow(k):
        kk = k[:, None, None].astype(F32)
        mag = jnp.exp(kk * (lr * dt)[None])
        ang = kk * (li * dt)[None]
        return mag * jnp.cos(ang), mag * jnp.sin(ang)

    abar_re, abar_im = apow(jnp.arange(1, 2))
    abar_re, abar_im = abar_re[0], abar_im[0]
    den = lr * lr + li * li
    nr, ni = abar_re - 1.0, abar_im
    q_re, q_im = (nr * lr + ni * li) / den, (ni * lr - nr * li) / den
    bb_re = q_re[..., None] * b_re - q_im[..., None] * b_im
    bb_im = q_re[..., None] * b_im + q_im[..., None] * b_re
    pw_re, pw_im = apow(jnp.arange(q + 1))
    ca_re = c_re[None] * pw_re[:, :, None, :] - c_im[None] * pw_im[:, :, None, :]
    ca_im = c_re[None] * pw_im[:, :, None, :] + c_im[None] * pw_re[:, :, None, :]
    kk = (jnp.einsum("kgip,gpj->kgij", ca_re[:q], bb_re, precision="highest")
          - jnp.einsum("kgip,gpj->kgij", ca_im[:q], bb_im, precision="highest"))
    s_idx = jnp.arange(q)[:, None]
    t_idx = jnp.arange(q)[None, :]
    lag = t_idx - s_idx
    same_g = jnp.eye(sg, dtype=bool)
    kk_t = kk.reshape(q, ns, sg, h_, h_).transpose(0, 1, 4, 2, 3)
    bd = jnp.where(same_g[None, None, :, None, :, None], kk_t[:, :, None], 0.0)
    bd = bd.reshape(q, ns, LANES, LANES).astype(BF16)
    blocks = jnp.where((lag >= 0)[:, :, None, None, None], bd[jnp.clip(lag, 0, q - 1)], 0)
    w_toep = blocks.transpose(2, 0, 3, 1, 4).reshape(ns, q * LANES, q * LANES)
    rp_re, rp_im = pw_re[:q][::-1], pw_im[:q][::-1]
    bs_re = rp_re[..., None] * bb_re[None] - rp_im[..., None] * bb_im[None]
    bs_im = rp_re[..., None] * bb_im[None] + rp_im[..., None] * bb_re[None]
    bsc = jnp.stack([bs_re, bs_im]).reshape(2, q, ns, sg, p_, h_)
    bsc = bsc.transpose(2, 1, 5, 0, 3, 4).astype(BF16)
    w_state = jnp.where(same_g[None, None, :, None, None, :, None], bsc[:, :, None], 0)
    w_state = w_state.reshape(ns, q * LANES, 2 * sg * p_)
    wc = jnp.concatenate([w_toep, w_state], axis=2)
    coc = jnp.stack([ca_re[1:], -ca_im[1:]]).reshape(2, q, ns, sg, h_, p_)
    coc = coc.transpose(2, 0, 5, 1, 3, 4).astype(BF16)
    wo = jnp.where(same_g[None, None, :, None, None, :, None], coc[:, :, None], 0)
    wo = wo.reshape(ns, 2 * sg * p_, q * LANES)
    nstep = max(1, int(math.ceil(math.log2(nchunk)))) if nchunk > 1 else 1
    sp_re, sp_im = apow(q * (2 ** jnp.arange(nstep)))
    nrow = 8 * ((nstep + 7) // 8)
    sp_re = sp_re.reshape(nstep, ns, sg * p_).transpose(1, 0, 2)
    sp_im = sp_im.reshape(nstep, ns, sg * p_).transpose(1, 0, 2)
    pad = ((0, 0), (0, nrow - nstep), (0, 0))
    ar = jnp.pad(jnp.concatenate([sp_re, sp_re], axis=-1), pad)
    ai = jnp.pad(jnp.concatenate([-sp_im, sp_im], axis=-1), pad)
    dflat = d_skip.reshape(ns, 1, LANES).astype(F32)
    return wc, wo, ar, ai, dflat


def _merge_kernel(h_ref, ys_ref, yl_ref, y5_ref, g0_ref, g1_ref, g2_ref, ws_ref, wl_ref, w5_ref, wo_ref, o_ref):
    d = D_MODEL
    p_ssd = jnp.dot(ys_ref[...], ws_ref[...], preferred_element_type=F32)
    p_lru = jnp.dot(yl_ref[...], wl_ref[...], preferred_element_type=F32)
    p_s5 = jnp.dot(y5_ref[...].astype(BF16), w5_ref[...], preferred_element_type=F32)
    y_s5 = p_s5[:, :d] * jax.nn.sigmoid(p_s5[:, d:])
    merged = (jax.nn.sigmoid(g0_ref[...].astype(F32)) * p_ssd
              + jax.nn.sigmoid(g1_ref[...].astype(F32)) * p_lru
              + jax.nn.sigmoid(g2_ref[...].astype(F32)) * y_s5)
    o_ref[...] = h_ref[...] + jnp.dot(merged.astype(BF16), wo_ref[...], preferred_element_type=F32)


def _merge(h, y_ssd, y_lru, y_s5, proj, w_ssd, w_lru, w_s5, w_out, tm=512):
    m = h.shape[0]
    d = D_MODEL
    row = lambda c: pl.BlockSpec((tm, d), lambda i, c=c: (i, c))
    full = lambda shape: pl.BlockSpec(shape, lambda i: (0, 0), pipeline_mode=SINGLE)
    return pl.pallas_call(
        _merge_kernel,
        grid=(m // tm,),
        in_specs=[row(0), row(0), row(0), row(0), row(COL_GATE0), row(COL_GATE0 + 1), row(COL_GATE0 + 2),
                  full((d, d)), full((d, d)), full((d, 2 * d)), full((d, d))],
        out_specs=row(0),
        out_shape=jax.ShapeDtypeStruct((m, d), F32),
        compiler_params=_cparams("parallel"),
        name="merge",
    )(h, y_ssd, y_lru, y_s5, proj, proj, proj, w_ssd, w_lru, w_s5, w_out)


def _ffn_kernel(h_ref, g_ref, wg_ref, wu_ref, wd_ref, o_ref, hn_ref, acc_ref):
    f = pl.program_id(1)

    @pl.when(f == 0)
    def _():
        hn_ref[...] = _rms(h_ref[...], g_ref[...]).astype(BF16)
        acc_ref[...] = jnp.zeros(acc_ref.shape, F32)

    hn = hn_ref[...]
    a = jnp.dot(hn, wg_ref[...], preferred_element_type=F32)
    b = jnp.dot(hn, wu_ref[...], preferred_element_type=F32)
    acc_ref[...] += jnp.dot((_silu(a) * b).astype(BF16), wd_ref[...], preferred_element_type=F32)

    @pl.when(f == pl.num_programs(1) - 1)
    def _():
        o_ref[...] = h_ref[...] + acc_ref[...]


def _ffn(h, g, wg, wu, wd, tm=512, tf=1408):
    m = h.shape[0]
    d = D_MODEL
    nf = wg.shape[1] // tf
    return pl.pallas_call(
        _ffn_kernel,
        grid=(m // tm, nf),
        in_specs=[
            pl.BlockSpec((tm, d), lambda i, f: (i, 0)),
            pl.BlockSpec((1, d), lambda i, f: (0, 0)),
            pl.BlockSpec((d, tf), lambda i, f: (0, f)),
            pl.BlockSpec((d, tf), lambda i, f: (0, f)),
            pl.BlockSpec((tf, d), lambda i, f: (f, 0)),
        ],
        out_specs=pl.BlockSpec((tm, d), lambda i, f: (i, 0)),
        out_shape=jax.ShapeDtypeStruct((m, d), F32),
        scratch_shapes=[pltpu.VMEM((tm, d), BF16), pltpu.VMEM((tm, d), F32)],
        compiler_params=_cparams("parallel", "arbitrary"),
        name="ffn_dense",
    )(h, g, wg, wu, wd)


def _router_kernel(h_ref, g_ref, wr_ref, br_ref, o_ref):
    hn = _rms(h_ref[...], g_ref[...])
    logits = jnp.dot(hn, wr_ref[...], preferred_element_type=F32, precision=lax.Precision.HIGHEST) + br_ref[...]
    lane = lax.broadcasted_iota(jnp.int32, logits.shape, 1)
    logits = jnp.where(lane < N_EXPERTS, logits, -jnp.inf)
    m1 = jnp.max(logits, axis=-1, keepdims=True)
    i1 = jnp.min(jnp.where(logits == m1, lane, LANES), axis=-1, keepdims=True)
    rest = jnp.where(lane == i1, -jnp.inf, logits)
    m2 = jnp.max(rest, axis=-1, keepdims=True)
    i2 = jnp.min(jnp.where(rest == m2, lane, LANES), axis=-1, keepdims=True)
    e2 = jnp.exp(m2 - m1)
    w1 = 1.0 / (1.0 + e2)
    w2 = e2 / (1.0 + e2)
    o_ref[...] = jnp.where(lane == 0, i1.astype(F32),
                           jnp.where(lane == 1, i2.astype(F32),
                                     jnp.where(lane == 2, w1, jnp.where(lane == 3, w2, 0.0))))


def _router(h, g, wr, br, tm=512):
    m = h.shape[0]
    d = D_MODEL
    return pl.pallas_call(
        _router_kernel,
        grid=(m // tm,),
        in_specs=[
            pl.BlockSpec((tm, d), lambda i: (i, 0)),
            pl.BlockSpec((1, d), lambda i: (0, 0)),
            pl.BlockSpec((d, LANES), lambda i: (0, 0)),
            pl.BlockSpec((1, LANES), lambda i: (0, 0)),
        ],
        out_specs=pl.BlockSpec((tm, LANES), lambda i: (i, 0)),
        out_shape=jax.ShapeDtypeStruct((m, LANES), F32),
        compiler_params=_cparams("parallel"),
        name="router",
    )(h, g, wr, br)


def _moe_kernel(nblk_ref, h_ref, g_ref, ri_ref, wg_ref, wu_ref, wd_ref, fg_ref, o_ref,
                hn_ref, rank_ref, rankt_ref, xg_ref, acc_ref, *, final, cap):
    i = pl.program_id(0)
    e = pl.program_id(1)
    f = pl.program_id(2)
    ne = pl.num_programs(1)
    nf = pl.num_programs(2)
    tm = h_ref.shape[0]
    nb = nblk_ref[i * ne + e]
    ef = e.astype(F32)

    @pl.when((e == 0) & (f == 0))
    def _():
        h = h_ref[...]
        hn_ref[...] = _rms(h, g_ref[...]).astype(BF16)
        o_ref[...] = h
        lane = lax.broadcasted_iota(jnp.int32, (RANK_BLK, LANES), 1).astype(F32)
        rr = lax.broadcasted_iota(jnp.int32, (RANK_BLK, RANK_BLK), 0)
        cc = lax.broadcasted_iota(jnp.int32, (RANK_BLK, RANK_BLK), 1)
        tril = jnp.where(rr >= cc, 1.0, 0.0).astype(BF16)
        carry = jnp.zeros((1, LANES), F32)
        for s in range(tm // RANK_BLK):
            ri = ri_ref[s * RANK_BLK:(s + 1) * RANK_BLK, :]
            mask = jnp.where((lane == ri[:, 0:1]) | (lane == ri[:, 1:2]), 1.0, 0.0)
            cnt = jnp.dot(tril, mask.astype(BF16), preferred_element_type=F32) + carry
            rank_ref[s * RANK_BLK:(s + 1) * RANK_BLK, :] = cnt * mask
            carry = cnt[RANK_BLK - 1:RANK_BLK, :]
        rankt_ref[...] = rank_ref[...].T

    def row_base(b):
        return pl.multiple_of(b * cap, 16)

    @pl.when(f == 0)
    def _():
        rt = rankt_ref[pl.ds(e, 1), :]
        hn = hn_ref[...]

        def gather_blk(b, c):
            xg = jnp.zeros((cap, D_MODEL), F32)
            for k0 in range(0, tm, MOE_SEL_CHUNK):
                rid = (lax.broadcasted_iota(jnp.int32, (cap, MOE_SEL_CHUNK), 0) + (b * cap + 1)).astype(F32)
                sel = jnp.where(rt[:, k0:k0 + MOE_SEL_CHUNK] == rid, 1.0, 0.0).astype(BF16)
                xg = xg + jnp.dot(sel, hn[k0:k0 + MOE_SEL_CHUNK, :], preferred_element_type=F32)
            xg_ref[pl.ds(row_base(b), cap), :] = xg.astype(BF16)
            acc_ref[pl.ds(row_base(b), cap), :] = jnp.zeros((cap, D_MODEL), F32)
            return c

        lax.fori_loop(0, nb, gather_blk, 0)

    def ffn_blk(b, c):
        x = xg_ref[pl.ds(row_base(b), cap), :]
        a = jnp.dot(x, wg_ref[0], preferred_element_type=F32)
        u = jnp.dot(x, wu_ref[0], preferred_element_type=F32)
        acc_ref[pl.ds(row_base(b), cap), :] += jnp.dot((_silu(a) * u).astype(BF16), wd_ref[0],
                                                       preferred_element_type=F32)
        return c

    lax.fori_loop(0, nb, ffn_blk, 0)

    @pl.when(f == nf - 1)
    def _():
        ri = ri_ref[...]
        we = jnp.where(ri[:, 0:1] == ef, ri[:, 2:3], 0.0) + jnp.where(ri[:, 1:2] == ef, ri[:, 3:4], 0.0)
        lane = lax.broadcasted_iota(jnp.int32, (tm, LANES), 1)
        rk = jnp.sum(jnp.where(lane == e, rank_ref[...], 0.0), axis=-1, keepdims=True)

        def scatter_blk(b, c):
            yb = acc_ref[pl.ds(row_base(b), cap), :].astype(BF16)
            for r0 in range(0, tm, MOE_SEL_CHUNK):
                cid = (lax.broadcasted_iota(jnp.int32, (MOE_SEL_CHUNK, cap), 1) + (b * cap + 1)).astype(F32)
                sel_t = jnp.where(rk[r0:r0 + MOE_SEL_CHUNK, :] == cid, 1.0, 0.0).astype(BF16)
                y = jnp.dot(sel_t, yb, preferred_element_type=F32)
                o_ref[r0:r0 + MOE_SEL_CHUNK, :] += we[r0:r0 + MOE_SEL_CHUNK, :] * y
            return c

        lax.fori_loop(0, nb, scatter_blk, 0)

        if final:
            @pl.when(e == ne - 1)
            def _():
                o_ref[...] = _rms(o_ref[...], fg_ref[...])


def _moe(h, g, rinfo, nblk, wg, wu, wd, fg, final, tf=896):
    m = h.shape[0]
    d = D_MODEL
    tm = min(MOE_TM, m)
    cap = MOE_CAP
    ne, _, fe = wg.shape
    max_rows = -(-tm // cap) * cap
    grid_spec = pltpu.PrefetchScalarGridSpec(
        num_scalar_prefetch=1,
        grid=(m // tm, ne, fe // tf),
        in_specs=[
            pl.BlockSpec((tm, d), lambda i, e, f, nb: (i, 0)),
            pl.BlockSpec((1, d), lambda i, e, f, nb: (0, 0)),
            pl.BlockSpec((tm, LANES), lambda i, e, f, nb: (i, 0)),
            pl.BlockSpec((1, d, tf), lambda i, e, f, nb: (e, 0, f)),
            pl.BlockSpec((1, d, tf), lambda i, e, f, nb: (e, 0, f)),
            pl.BlockSpec((1, tf, d), lambda i, e, f, nb: (e, f, 0)),
            pl.BlockSpec((1, d), lambda i, e, f, nb: (0, 0)),
        ],
        out_specs=pl.BlockSpec((tm, d), lambda i, e, f, nb: (i, 0)),
        scratch_shapes=[
            pltpu.VMEM((tm, d), BF16),
            pltpu.VMEM((tm, LANES), F32),
            pltpu.VMEM((LANES, tm), F32),
            pltpu.VMEM((max_rows, d), BF16),
            pltpu.VMEM((max_rows, d), F32),
        ],
    )
    return pl.pallas_call(
        functools.partial(_moe_kernel, final=final, cap=cap),
        grid_spec=grid_spec,
        out_shape=jax.ShapeDtypeStruct((m, d), F32),
        compiler_params=pltpu.CompilerParams(dimension_semantics=("parallel", "arbitrary", "arbitrary"),
                                             vmem_limit_bytes=MOE_VMEM_LIMIT),
        name="moe_experts",
    )(nblk, h, g, rinfo, wg, wu, wd, fg)


def _moe_block_counts(rinfo, tm, cap):
    m = rinfo.shape[0]
    ids = rinfo[:, 0:2].astype(jnp.int32)
    onehot = (ids[:, :, None] == jnp.arange(N_EXPERTS, dtype=jnp.int32)).any(axis=1)
    counts = onehot.reshape(m // tm, tm, N_EXPERTS).astype(jnp.int32).sum(axis=1)
    return ((counts + cap - 1) // cap).reshape(-1)


def _final_norm_kernel(h_ref, g_ref, o_ref):
    o_ref[...] = _rms(h_ref[...], g_ref[...])


def _final_norm(h, g, tm=1024):
    m, d = h.shape
    return pl.pallas_call(
        _final_norm_kernel,
        grid=(m // tm,),
        in_specs=[pl.BlockSpec((tm, d), lambda i: (i, 0)), pl.BlockSpec((1, d), lambda i: (0, 0))],
        out_specs=pl.BlockSpec((tm, d), lambda i: (i, 0)),
        out_shape=jax.ShapeDtypeStruct((m, d), F32),
        compiler_params=_cparams("parallel"),
        name="final_norm",
    )(h, g)


def _pack_w_in(w):
    z = w[:, 0:1024]
    xs = w[:, 1024:2048]
    bc = w[:, 2048:2560]
    dt = w[:, 2560:2576]
    lg = w[:, 2576:3600]
    lx = w[:, 3600:4624]
    s5 = w[:, 4624:5648]
    gates = w[:, 5648:8720]
    packed = jnp.concatenate([s5, z, xs, lg, lx, gates, bc], axis=1).astype(BF16)
    wdt = jnp.pad(dt, ((0, 0), (0, LANES - SSD_HEADS))).astype(BF16)
    return packed, wdt


def _block_diag_gates(w_a, w_x, bw=256):
    per = bw // LRU_HEAD_DIM
    nblk = LRU_WIDTH // bw
    eye = jnp.eye(per, dtype=F32)

    def bd(w):
        w = w.reshape(nblk, per, LRU_HEAD_DIM, LRU_HEAD_DIM)
        return jnp.einsum("sbij,bc->sbicj", w, eye).reshape(nblk, bw, bw)

    return jnp.concatenate([bd(w_a), bd(w_x)], axis=2).astype(BF16)


def _row(v, width=None):
    v = v.reshape(1, -1).astype(F32)
    if width is not None and v.shape[1] < width:
        v = jnp.pad(v, ((0, 0), (0, width - v.shape[1])))
    return v


def kernel(x, mix_norm, w_in, ssd_conv_w, ssd_conv_b, ssd_dt_bias, ssd_a_log, ssd_d, ssd_norm, w_branch_ssd, lru_conv_w, lru_conv_b, lru_w_a, lru_b_a, lru_w_x, lru_b_x, lru_lambda, w_branch_lru, s5_lam_re, s5_lam_im, s5_b_re, s5_b_im, s5_c_re, s5_c_im, s5_d, s5_log_dt, w_branch_s5, w_out, ffn_norm, dense_w_gate, dense_w_up, dense_w_down, moe_router, moe_router_bias, moe_w_gate, moe_w_up, moe_w_down, final_norm):
    batch, seq, d = x.shape
    depth = mix_norm.shape[0]
    m = batch * seq
    nchunk5 = seq // S5_CHUNK
    h = x.reshape(m, d)
    normed = False
    for layer in range(depth):
        w_pack, w_dt = _pack_w_in(w_in[layer])
        proj, u5, dt_raw = _inproj(h, _row(mix_norm[layer]), w_pack, w_dt)

        cw = ssd_conv_w[layer]
        cb = ssd_conv_b[layer]
        alog = jnp.pad(ssd_a_log[layer].reshape(1, -1), ((0, 0), (0, LANES - SSD_HEADS)), constant_values=-jnp.inf)
        y_ssd = _ssd(proj, dt_raw, cw[:, :SSD_INNER], _row(cb[:SSD_INNER]), cw[:, SSD_INNER:], _row(cb[SSD_INNER:]),
                     _row(ssd_dt_bias[layer], LANES), alog, _row(jnp.repeat(ssd_d[layer], SSD_HEAD_DIM)),
                     _row(ssd_norm[layer]), batch, seq)

        y_lru = _lru(proj, lru_conv_w[layer], _row(lru_conv_b[layer]),
                     _block_diag_gates(lru_w_a[layer], lru_w_x[layer]), _row(lru_b_a[layer]),
                     _row(lru_b_x[layer]), _row(lru_lambda[layer]), batch, seq)

        wc5, wo5, ar, ai, dflat = _s5_matrices(
            s5_lam_re[layer], s5_lam_im[layer], s5_b_re[layer], s5_b_im[layer], s5_c_re[layer], s5_c_im[layer],
            s5_d[layer], s5_log_dt[layer], nchunk5)
        y_s5 = _s5(u5, wc5, wo5, ar, ai, dflat, nchunk5)

        h = _merge(h, y_ssd, y_lru, y_s5, proj, w_branch_ssd[layer].astype(BF16), w_branch_lru[layer].astype(BF16),
                   w_branch_s5[layer].astype(BF16), w_out[layer].astype(BF16))

        j = layer // 2
        if layer % 2 == 0:
            h = _ffn(h, _row(ffn_norm[layer]), dense_w_gate[j].astype(BF16), dense_w_up[j].astype(BF16),
                     dense_w_down[j].astype(BF16))
        else:
            gn = _row(ffn_norm[layer])
            wr = jnp.pad(moe_router[j], ((0, 0), (0, LANES - N_EXPERTS)))
            rinfo = _router(h, gn, wr, _row(moe_router_bias[j], LANES))
            nblk = _moe_block_counts(rinfo, min(MOE_TM, m), MOE_CAP)
            normed = layer == depth - 1
            h = _moe(h, gn, rinfo, nblk, moe_w_gate[j].astype(BF16), moe_w_up[j].astype(BF16),
                     moe_w_down[j].astype(BF16), _row(final_norm), normed)
    if not normed:
        h = _final_norm(h, _row(final_norm))
    return h.reshape(batch, seq, d)
```

```python
import functools
import math

import jax
import jax.numpy as jnp
from jax import lax
from jax.experimental import pallas as pl
from jax.experimental.pallas import tpu as pltpu

F32 = jnp.float32
BF16 = jnp.bfloat16

D_MODEL = 1024
SSD_HEADS = 16
SSD_HEAD_DIM = 64
SSD_INNER = 1024
SSD_GROUPS = 2
SSD_STATE = 128
SSD_CHUNK = 256
CONV_WIDTH = 4
LRU_WIDTH = 1024
LRU_HEADS = 16
LRU_HEAD_DIM = 64
LRU_C = 8.0
S5_WIDTH = 1024
S5_GROUP = 16
S5_GROUPS = 64
S5_STATE = 64
S5_CHUNK = 16
S5_SLAB_GROUPS = 8
N_EXPERTS = 8
EPS = 1e-6

LANES = 128
SUBLANES = 8
CONV_PAD = 8

COL_Z, COL_XS, COL_LG, COL_LX, COL_GATE0 = 0, 1, 2, 3, 4
COL_BC_512 = 14
INPROJ_TN = 512
S5_TILES = S5_WIDTH // INPROJ_TN

MOE_TM = 1024
MOE_CAP = 304
MOE_SEL_CHUNK = 512
MOE_VMEM_LIMIT = 58 * 1024 * 1024
RANK_BLK = 256

VMEM_LIMIT = 52 * 1024 * 1024
SINGLE = pl.Buffered(1)


def _cparams(*sem):
    return pltpu.CompilerParams(dimension_semantics=sem, vmem_limit_bytes=VMEM_LIMIT)


def _rms(x, g):
    return x * lax.rsqrt(jnp.mean(x * x, axis=-1, keepdims=True) + EPS) * g


def _softplus(x):
    return jnp.maximum(x, 0.0) + jnp.log1p(jnp.exp(-jnp.abs(x)))


def _silu(x):
    return x * jax.nn.sigmoid(x)


def _inproj_kernel(x_ref, g_ref, w_ref, wdt_ref, o_ref, u5_ref, dt_ref, xn_ref):
    j = pl.program_id(1)

    @pl.when(j == 0)
    def _():
        xn = _rms(x_ref[...], g_ref[...]).astype(BF16)
        xn_ref[...] = xn
        dt_ref[...] = jnp.dot(xn, wdt_ref[...], preferred_element_type=F32)

    res = jnp.dot(xn_ref[...], w_ref[...], preferred_element_type=F32)

    @pl.when(j < S5_TILES)
    def _():
        u5_ref[...] = res

    @pl.when(j >= S5_TILES)
    def _():
        o_ref[...] = res.astype(o_ref.dtype)


def _inproj(h, g, w, wdt, tm=2048):
    m = h.shape[0]
    tm = min(tm, m)
    tn = INPROJ_TN
    n = w.shape[1]
    return pl.pallas_call(
        _inproj_kernel,
        grid=(m // tm, n // tn),
        in_specs=[
            pl.BlockSpec((tm, D_MODEL), lambda i, j: (i, 0)),
            pl.BlockSpec((1, D_MODEL), lambda i, j: (0, 0)),
            pl.BlockSpec((D_MODEL, tn), lambda i, j: (0, j)),
            pl.BlockSpec((D_MODEL, LANES), lambda i, j: (0, 0)),
        ],
        out_specs=[
            pl.BlockSpec((tm, tn), lambda i, j: (i, jnp.maximum(j - S5_TILES, 0))),
            pl.BlockSpec((tm, tn), lambda i, j: (i, jnp.minimum(j, S5_TILES - 1))),
            pl.BlockSpec((tm, LANES), lambda i, j: (i, 0)),
        ],
        out_shape=[
            jax.ShapeDtypeStruct((m, n - S5_WIDTH), BF16),
            jax.ShapeDtypeStruct((m, S5_WIDTH), F32),
            jax.ShapeDtypeStruct((m, LANES), F32),
        ],
        scratch_shapes=[pltpu.VMEM((tm, D_MODEL), BF16)],
        compiler_params=_cparams("parallel", "arbitrary"),
        name="inproj",
    )(h, g, w, wdt)


def _conv_chunk(ext_ref, x_f32, w_ref, b_ref, first):
    t = x_f32.shape[0]

    @pl.when(first)
    def _():
        ext_ref[0:CONV_PAD, :] = jnp.zeros((CONV_PAD, ext_ref.shape[1]), F32)

    ext_ref[CONV_PAD:CONV_PAD + t, :] = x_f32
    acc = b_ref[...] + w_ref[CONV_WIDTH - 1:CONV_WIDTH, :] * x_f32
    for k in range(CONV_WIDTH - 1):
        off = CONV_PAD - (CONV_WIDTH - 1) + k
        acc = acc + w_ref[k:k + 1, :] * ext_ref[off:off + t, :]
    ext_ref[CONV_PAD - (CONV_WIDTH - 1):CONV_PAD, :] = ext_ref[CONV_PAD + t - (CONV_WIDTH - 1):CONV_PAD + t, :]
    return acc


def _expand_heads(v, nslab):
    r = v.shape[0]
    lane = lax.broadcasted_iota(jnp.int32, (r, LANES), 1)
    slabs = []
    for k in range(nslab):
        lo = jnp.broadcast_to(v[:, 2 * k:2 * k + 1], (r, LANES))
        hi = jnp.broadcast_to(v[:, 2 * k + 1:2 * k + 2], (r, LANES))
        slabs.append(jnp.where(lane < SSD_HEAD_DIM, lo, hi))
    return jnp.concatenate(slabs, axis=1)


def _ssd_kernel(z_ref, xs_ref, bc_ref, dt_ref, wxs_ref, bxs_ref, wbc_ref, bbc_ref, dtb_ref, alog_ref,
                dskip_ref, ng_ref, o_ref, ext_xs, ext_bc, state_ref):
    q = SSD_CHUNK
    n = SSD_STATE
    gw = SSD_INNER // SSD_GROUPS
    first = pl.program_id(1) == 0

    @pl.when(first)
    def _():
        state_ref[...] = jnp.zeros(state_ref.shape, F32)

    xs = _silu(_conv_chunk(ext_xs, xs_ref[...].astype(F32), wxs_ref, bxs_ref, first))
    bc = _silu(_conv_chunk(ext_bc, bc_ref[...].astype(F32), wbc_ref, bbc_ref, first))

    dt = _softplus(dt_ref[...] + dtb_ref[...])
    a = -jnp.exp(alog_ref[...])
    da = dt * a
    row = lax.broadcasted_iota(jnp.int32, (q, q), 0)
    col = lax.broadcasted_iota(jnp.int32, (q, q), 1)
    causal = row >= col
    tril = causal.astype(F32)
    cs = jnp.dot(tril, da, preferred_element_type=F32, precision=lax.Precision.HIGHEST)
    cs_t = cs.T
    tot = cs[q - 1:q, :]

    xdt = xs * _expand_heads(dt, 8)
    xdt_b = xdt.astype(BF16)
    xdec_b = (xdt * _expand_heads(jnp.exp(tot - cs), 8)).astype(BF16)
    ecs = _expand_heads(jnp.exp(cs), 8)
    etot = _expand_heads(jnp.exp(tot), 8)
    lane = lax.broadcasted_iota(jnp.int32, (q, LANES), 1)

    y_groups = []
    for g in range(SSD_GROUPS):
        bg = bc[:, g * n:(g + 1) * n]
        cg = bc[:, SSD_GROUPS * n + g * n:SSD_GROUPS * n + (g + 1) * n].astype(BF16)
        cb = lax.dot_general(cg, bg.astype(BF16), (((1,), (1,)), ((), ())), preferred_element_type=F32)
        slabs = []
        for k in range(gw // LANES):
            res = []
            for hh in range(2):
                h = g * (SSD_HEADS // SSD_GROUPS) + 2 * k + hh
                seg = cs[:, h:h + 1] - cs_t[h:h + 1, :]
                lmat = jnp.exp(jnp.where(causal, seg, -jnp.inf))
                mh = (cb * lmat).astype(BF16)
                sl = g * gw + k * LANES
                res.append(jnp.dot(mh, xdt_b[:, sl:sl + LANES], preferred_element_type=F32))
            slabs.append(jnp.where(lane < SSD_HEAD_DIM, res[0], res[1]))
        y_diag = jnp.concatenate(slabs, axis=1)
        st = state_ref[g]
        y_off = jnp.dot(cg, st.astype(BF16), preferred_element_type=F32) * ecs[:, g * gw:(g + 1) * gw]
        upd = jnp.dot(bg.T.astype(BF16), xdec_b[:, g * gw:(g + 1) * gw], preferred_element_type=F32)
        state_ref[g] = st * etot[:, g * gw:(g + 1) * gw] + upd
        yg = y_diag + y_off + xs[:, g * gw:(g + 1) * gw] * dskip_ref[:, g * gw:(g + 1) * gw]
        yg = yg * _silu(z_ref[:, g * gw:(g + 1) * gw].astype(F32))
        yg = yg * lax.rsqrt(jnp.mean(yg * yg, axis=-1, keepdims=True) + EPS)
        y_groups.append(yg)
    y = jnp.concatenate(y_groups, axis=1) * ng_ref[...]
    o_ref[...] = y.astype(o_ref.dtype)


def _ssd(proj, dt_raw, wxs, bxs, wbc, bbc, dtb, alog, dskip, ng, batch, seq):
    q = SSD_CHUNK
    nc = seq // q
    m = batch * seq
    full = lambda shape: pl.BlockSpec(shape, lambda b, c: (0,) * len(shape))
    return pl.pallas_call(
        _ssd_kernel,
        grid=(batch, nc),
        in_specs=[
            pl.BlockSpec((q, 1024), lambda b, c: (b * nc + c, COL_Z)),
            pl.BlockSpec((q, 1024), lambda b, c: (b * nc + c, COL_XS)),
            pl.BlockSpec((q, 512), lambda b, c: (b * nc + c, COL_BC_512)),
            pl.BlockSpec((q, LANES), lambda b, c: (b * nc + c, 0)),
            full((CONV_WIDTH, 1024)), full((1, 1024)), full((CONV_WIDTH, 512)), full((1, 512)),
            full((1, LANES)), full((1, LANES)), full((1, 1024)), full((1, 1024)),
        ],
        out_specs=pl.BlockSpec((q, 1024), lambda b, c: (b * nc + c, 0)),
        out_shape=jax.ShapeDtypeStruct((m, SSD_INNER), BF16),
        scratch_shapes=[
            pltpu.VMEM((CONV_PAD + q, 1024), F32),
            pltpu.VMEM((CONV_PAD + q, 512), F32),
            pltpu.VMEM((SSD_GROUPS, SSD_STATE, SSD_INNER // SSD_GROUPS), F32),
        ],
        compiler_params=_cparams("parallel", "arbitrary"),
        name="ssd",
    )(proj, proj, proj, dt_raw, wxs, bxs, wbc, bbc, dtb, alog, dskip, ng)


def _lru_kernel(lg_ref, lx_ref, cw_ref, cb_ref, wbd_ref, ba_ref, bx_ref, lam_ref, o_ref, ext, carry_ref):
    t = lx_ref.shape[0]
    first = pl.program_id(1) == 0

    @pl.when(first)
    def _():
        carry_ref[...] = jnp.zeros(carry_ref.shape, F32)

    xc = _conv_chunk(ext, lx_ref[...].astype(F32), cw_ref, cb_ref, first)
    xcb = xc.astype(BF16)
    nblk = wbd_ref.shape[0]
    bw = LRU_WIDTH // nblk
    r_parts, i_parts = [], []
    for s in range(nblk):
        ga = jnp.dot(xcb[:, s * bw:(s + 1) * bw], wbd_ref[s], preferred_element_type=F32)
        r_parts.append(ga[:, :bw])
        i_parts.append(ga[:, bw:])
    r = jax.nn.sigmoid(jnp.concatenate(r_parts, axis=1) + ba_ref[...])
    i = jax.nn.sigmoid(jnp.concatenate(i_parts, axis=1) + bx_ref[...])
    log_a = (-LRU_C) * r * _softplus(-lam_ref[...])
    a = jnp.exp(log_a)
    u = jnp.sqrt(-jnp.tanh(log_a) * (a * a + 1.0)) * (i * xc)

    sub = lax.broadcasted_iota(jnp.int32, (t, LRU_WIDTH), 0) % SUBLANES
    sh = 1
    while sh < SUBLANES:
        keep = sub >= sh
        a_s = jnp.where(keep, pltpu.roll(a, sh, 0), 1.0)
        u_s = jnp.where(keep, pltpu.roll(u, sh, 0), 0.0)
        u = a * u_s + u
        a = a * a_s
        sh *= 2
    carry = carry_ref[...]
    tiles = []
    for k in range(t // SUBLANES):
        blk = u[k * SUBLANES:(k + 1) * SUBLANES, :] + a[k * SUBLANES:(k + 1) * SUBLANES, :] * carry
        carry = blk[SUBLANES - 1:SUBLANES, :]
        tiles.append(blk)
    h = jnp.concatenate(tiles, axis=0)
    carry_ref[...] = carry
    o_ref[...] = (h * jax.nn.gelu(lg_ref[...].astype(F32))).astype(o_ref.dtype)


def _lru(proj, cw, cb, wbd, ba, bx, lam, batch, seq, tc=256):
    nc = seq // tc
    m = batch * seq
    full = lambda shape: pl.BlockSpec(shape, lambda b, c: (0,) * len(shape))
    return pl.pallas_call(
        _lru_kernel,
        grid=(batch, nc),
        in_specs=[
            pl.BlockSpec((tc, 1024), lambda b, c: (b * nc + c, COL_LG)),
            pl.BlockSpec((tc, 1024), lambda b, c: (b * nc + c, COL_LX)),
            full((CONV_WIDTH, 1024)), full((1, 1024)), full(wbd.shape),
            full((1, 1024)), full((1, 1024)), full((1, 1024)),
        ],
        out_specs=pl.BlockSpec((tc, 1024), lambda b, c: (b * nc + c, 0)),
        out_shape=jax.ShapeDtypeStruct((m, LRU_WIDTH), BF16),
        scratch_shapes=[pltpu.VMEM((CONV_PAD + tc, 1024), F32), pltpu.VMEM((1, LRU_WIDTH), F32)],
        compiler_params=_cparams("parallel", "arbitrary"),
        name="rglru",
    )(proj, proj, cw, cb, wbd, ba, bx, lam)


def _s5_kernel(x_ref, wc_ref, wo_ref, ar_ref, ai_ref, d_ref, o_ref, *, nchunk):
    q = S5_CHUNK
    r = x_ref.shape[0] // q
    nst = S5_SLAB_GROUPS * S5_STATE
    xs = [x_ref[pl.ds(t, r, stride=q), :] for t in range(q)]
    acat = jnp.concatenate([x.astype(BF16) for x in xs], axis=1)
    res = jnp.dot(acat, wc_ref[0], preferred_element_type=F32)
    hs = res[:, q * LANES:]
    cidx = lax.broadcasted_iota(jnp.int32, (r, 2 * nst), 0) % nchunk
    ar = ar_ref[0]
    ai = ai_ref[0]
    j = 0
    sh = 1
    while sh < nchunk:
        prev = jnp.where(cidx >= sh, pltpu.roll(hs, sh, 0), 0.0)
        hs = hs + ar[j:j + 1, :] * prev + ai[j:j + 1, :] * pltpu.roll(prev, nst, 1)
        sh *= 2
        j += 1
    h_in = jnp.where(cidx >= 1, pltpu.roll(hs, 1, 0), 0.0)
    yoff = jnp.dot(h_in.astype(BF16), wo_ref[0], preferred_element_type=F32)
    d = d_ref[0]
    for t in range(q):
        y = res[:, t * LANES:(t + 1) * LANES] + yoff[:, t * LANES:(t + 1) * LANES] + xs[t] * d
        o_ref[pl.ds(t, r, stride=q), :] = jax.nn.gelu(y)


def _s5(u5, wc, wo, ar, ai, dflat, nchunk, rt=512):
    m = u5.shape[0]
    q = S5_CHUNK
    rt = min(rt, m // q)
    per_a = lambda a: pl.BlockSpec((1,) + a.shape[1:], lambda s, i: (s, 0, 0), pipeline_mode=SINGLE)
    return pl.pallas_call(
        functools.partial(_s5_kernel, nchunk=nchunk),
        grid=(S5_WIDTH // LANES, m // (q * rt)),
        in_specs=[pl.BlockSpec((q * rt, LANES), lambda s, i: (i, s)),
                  per_a(wc), per_a(wo), per_a(ar), per_a(ai), per_a(dflat)],
        out_specs=pl.BlockSpec((q * rt, LANES), lambda s, i: (i, s)),
        out_shape=jax.ShapeDtypeStruct((m, S5_WIDTH), F32),
        compiler_params=_cparams("parallel", "parallel"),
        name="s5",
    )(u5, wc, wo, ar, ai, dflat)


def _s5_matrices(lam_re, lam_im, b_re, b_im, c_re, c_im, d_skip, log_dt, nchunk):
    q = S5_CHUNK
    g_, p_, h_ = S5_GROUPS, S5_STATE, S5_GROUP
    ns, sg = S5_WIDTH // LANES, S5_SLAB_GROUPS
    dt = jnp.exp(log_dt)[:, None]
    lr, li = lam_re, lam_im

    def apow(k):
        kk = k[:, None, None].astype(F32)
        mag = jnp.exp(kk * (lr * dt)[None])
        ang = kk * (li * dt)[None]
        return mag * jnp.cos(ang), mag * jnp.sin(ang)

    abar_re, abar_im = apow(jnp.arange(1, 2))
    abar_re, abar_im = abar_re[0], abar_im[0]
    den = lr * lr + li * li
    nr, ni = abar_re - 1.0, abar_im
    q_re, q_im = (nr * lr + ni * li) / den, (ni * lr - nr * li) / den
    bb_re = q_re[..., None] * b_re - q_im[..., None] * b_im
    bb_im = q_re[..., None] * b_im + q_im[..., None] * b_re
    pw_re, pw_im = apow(jnp.arange(q + 1))
    ca_re = c_re[None] * pw_re[:, :, None, :] - c_im[None] * pw_im[:, :, None, :]
    ca_im = c_re[None] * pw_im[:, :, None, :] + c_im[None] * pw_re[:, :, None, :]
    kk = (jnp.einsum("kgip,gpj->kgij", ca_re[:q], bb_re, precision="highest")
          - jnp.einsum("kgip,gpj->kgij", ca_im[:q], bb_im, precision="highest"))
    s_idx = jnp.arange(q)[:, None]
    t_idx = jnp.arange(q)[None, :]
    lag = t_idx - s_idx
    same_g = jnp.eye(sg, dtype=bool)
    kk_t = kk.reshape(q, ns, sg, h_, h_).transpose(0, 1, 4, 2, 3)
    bd = jnp.where(same_g[None, None, :, None, :, None], kk_t[:, :, None], 0.0)
    bd = bd.reshape(q, ns, LANES, LANES).astype(BF16)
    blocks = jnp.where((lag >= 0)[:, :, None, None, None], bd[jnp.clip(lag, 0, q - 1)], 0)
    w_toep = blocks.transpose(2, 0, 3, 1, 4).reshape(ns, q * LANES, q * LANES)
    rp_re, rp_im = pw_re[:q][::-1], pw_im[:q][::-1]
    bs_re = rp_re[..., None] * bb_re[None] - rp_im[..., None] * bb_im[None]
    bs_im = rp_re[..., None] * bb_im[None] + rp_im[..., None] * bb_re[None]
    bsc = jnp.stack([bs_re, bs_im]).reshape(2, q, ns, sg, p_, h_)
    bsc = bsc.transpose(2, 1, 5, 0, 3, 4).astype(BF16)
    w_state = jnp.where(same_g[None, None, :, None, None, :, None], bsc[:, :, None], 0)
    w_state = w_state.reshape(ns, q * LANES, 2 * sg * p_)
    wc = jnp.concatenate([w_toep, w_state], axis=2)
    coc = jnp.stack([ca_re[1:], -ca_im[1:]]).reshape(2, q, ns, sg, h_, p_)
    coc = coc.transpose(2, 0, 5, 1, 3, 4).astype(BF16)
    wo = jnp.where(same_g[None, None, :, None, None, :, None], coc[:, :, None], 0)
    wo = wo.reshape(ns, 2 * sg * p_, q * LANES)
    nstep = max(1, int(math.ceil(math.log2(nchunk)))) if nchunk > 1 else 1
    sp_re, sp_im = apow(q * (2 ** jnp.arange(nstep)))
    nrow = 8 * ((nstep + 7) // 8)
    sp_re = sp_re.reshape(nstep, ns, sg * p_).transpose(1, 0, 2)
    sp_im = sp_im.reshape(nstep, ns, sg * p_).transpose(1, 0, 2)
    pad = ((0, 0), (0, nrow - nstep), (0, 0))
    ar = jnp.pad(jnp.concatenate([sp_re, sp_re], axis=-1), pad)
    ai = jnp.pad(jnp.concatenate([-sp_im, sp_im], axis=-1), pad)
    dflat = d_skip.reshape(ns, 1, LANES).astype(F32)
    return wc, wo, ar, ai, dflat


def _merge_kernel(h_ref, ys_ref, yl_ref, y5_ref, g0_ref, g1_ref, g2_ref, ws_ref, wl_ref, w5_ref, wo_ref, o_ref):
    d = D_MODEL
    p_ssd = jnp.dot(ys_ref[...], ws_ref[...], preferred_element_type=F32)
    p_lru = jnp.dot(yl_ref[...], wl_ref[...], preferred_element_type=F32)
    p_s5 = jnp.dot(y5_ref[...].astype(BF16), w5_ref[...], preferred_element_type=F32)
    y_s5 = p_s5[:, :d] * jax.nn.sigmoid(p_s5[:, d:])
    merged = (jax.nn.sigmoid(g0_ref[...].astype(F32)) * p_ssd
              + jax.nn.sigmoid(g1_ref[...].astype(F32)) * p_lru
              + jax.nn.sigmoid(g2_ref[...].astype(F32)) * y_s5)
    o_ref[...] = h_ref[...] + jnp.dot(merged.astype(BF16), wo_ref[...], preferred_element_type=F32)


def _merge(h, y_ssd, y_lru, y_s5, proj, w_ssd, w_lru, w_s5, w_out, tm=512):
    m = h.shape[0]
    d = D_MODEL
    row = lambda c: pl.BlockSpec((tm, d), lambda i, c=c: (i, c))
    full = lambda shape: pl.BlockSpec(shape, lambda i: (0, 0), pipeline_mode=SINGLE)
    return pl.pallas_call(
        _merge_kernel,
        grid=(m // tm,),
        in_specs=[row(0), row(0), row(0), row(0), row(COL_GATE0), row(COL_GATE0 + 1), row(COL_GATE0 + 2),
                  full((d, d)), full((d, d)), full((d, 2 * d)), full((d, d))],
        out_specs=row(0),
        out_shape=jax.ShapeDtypeStruct((m, d), F32),
        compiler_params=_cparams("parallel"),
        name="merge",
    )(h, y_ssd, y_lru, y_s5, proj, proj, proj, w_ssd, w_lru, w_s5, w_out)


def _ffn_kernel(h_ref, g_ref, wg_ref, wu_ref, wd_ref, o_ref, hn_ref, acc_ref):
    f = pl.program_id(1)

    @pl.when(f == 0)
    def _():
        hn_ref[...] = _rms(h_ref[...], g_ref[...]).astype(BF16)
        acc_ref[...] = jnp.zeros(acc_ref.shape, F32)

    hn = hn_ref[...]
    a = jnp.dot(hn, wg_ref[...], preferred_element_type=F32)
    b = jnp.dot(hn, wu_ref[...], preferred_element_type=F32)
    acc_ref[...] += jnp.dot((_silu(a) * b).astype(BF16), wd_ref[...], preferred_element_type=F32)

    @pl.when(f == pl.num_programs(1) - 1)
    def _():
        o_ref[...] = h_ref[...] + acc_ref[...]


def _ffn(h, g, wg, wu, wd, tm=512, tf=1408):
    m = h.shape[0]
    d = D_MODEL
    nf = wg.shape[1] // tf
    return pl.pallas_call(
        _ffn_kernel,
        grid=(m // tm, nf),
        in_specs=[
            pl.BlockSpec((tm, d), lambda i, f: (i, 0)),
            pl.BlockSpec((1, d), lambda i, f: (0, 0)),
            pl.BlockSpec((d, tf), lambda i, f: (0, f)),
            pl.BlockSpec((d, tf), lambda i, f: (0, f)),
            pl.BlockSpec((tf, d), lambda i, f: (f, 0)),
        ],
        out_specs=pl.BlockSpec((tm, d), lambda i, f: (i, 0)),
        out_shape=jax.ShapeDtypeStruct((m, d), F32),
        scratch_shapes=[pltpu.VMEM((tm, d), BF16), pltpu.VMEM((tm, d), F32)],
        compiler_params=_cparams("parallel", "arbitrary"),
        name="ffn_dense",
    )(h, g, wg, wu, wd)


def _router_kernel(h_ref, g_ref, wr_ref, br_ref, o_ref):
    hn = _rms(h_ref[...], g_ref[...])
    logits = jnp.dot(hn, wr_ref[...], preferred_element_type=F32, precision=lax.Precision.HIGHEST) + br_ref[...]
    lane = lax.broadcasted_iota(jnp.int32, logits.shape, 1)
    logits = jnp.where(lane < N_EXPERTS, logits, -jnp.inf)
    m1 = jnp.max(logits, axis=-1, keepdims=True)
    i1 = jnp.min(jnp.where(logits == m1, lane, LANES), axis=-1, keepdims=True)
    rest = jnp.where(lane == i1, -jnp.inf, logits)
    m2 = jnp.max(rest, axis=-1, keepdims=True)
    i2 = jnp.min(jnp.where(rest == m2, lane, LANES), axis=-1, keepdims=True)
    e2 = jnp.exp(m2 - m1)
    w1 = 1.0 / (1.0 + e2)
    w2 = e2 / (1.0 + e2)
    o_ref[...] = jnp.where(lane == 0, i1.astype(F32),
                           jnp.where(lane == 1, i2.astype(F32),
                                     jnp.where(lane == 2, w1, jnp.where(lane == 3, w2, 0.0))))


def _router(h, g, wr, br, tm=512):
    m = h.shape[0]
    d = D_MODEL
    return pl.pallas_call(
        _router_kernel,
        grid=(m // tm,),
        in_specs=[
            pl.BlockSpec((tm, d), lambda i: (i, 0)),
            pl.BlockSpec((1, d), lambda i: (0, 0)),
            pl.BlockSpec((d, LANES), lambda i: (0, 0)),
            pl.BlockSpec((1, LANES), lambda i: (0, 0)),
        ],
        out_specs=pl.BlockSpec((tm, LANES), lambda i: (i, 0)),
        out_shape=jax.ShapeDtypeStruct((m, LANES), F32),
        compiler_params=_cparams("parallel"),
        name="router",
    )(h, g, wr, br)


def _moe_kernel(nblk_ref, h_ref, g_ref, ri_ref, wg_ref, wu_ref, wd_ref, fg_ref, o_ref,
                hn_ref, rank_ref, rankt_ref, xg_ref, acc_ref, *, final, cap):
    i = pl.program_id(0)
    e = pl.program_id(1)
    f = pl.program_id(2)
    ne = pl.num_programs(1)
    nf = pl.num_programs(2)
    tm = h_ref.shape[0]
    nb = nblk_ref[i * ne + e]
    ef = e.astype(F32)

    @pl.when((e == 0) & (f == 0))
    def _():
        h = h_ref[...]
        hn_ref[...] = _rms(h, g_ref[...]).astype(BF16)
        o_ref[...] = h
        lane = lax.broadcasted_iota(jnp.int32, (RANK_BLK, LANES), 1).astype(F32)
        rr = lax.broadcasted_iota(jnp.int32, (RANK_BLK, RANK_BLK), 0)
        cc = lax.broadcasted_iota(jnp.int32, (RANK_BLK, RANK_BLK), 1)
        tril = jnp.where(rr >= cc, 1.0, 0.0).astype(BF16)
        carry = jnp.zeros((1, LANES), F32)
        for s in range(tm // RANK_BLK):
            ri = ri_ref[s * RANK_BLK:(s + 1) * RANK_BLK, :]
            mask = jnp.where((lane == ri[:, 0:1]) | (lane == ri[:, 1:2]), 1.0, 0.0)
            cnt = jnp.dot(tril, mask.astype(BF16), preferred_element_type=F32) + carry
            rank_ref[s * RANK_BLK:(s + 1) * RANK_BLK, :] = cnt * mask
            carry = cnt[RANK_BLK - 1:RANK_BLK, :]
        rankt_ref[...] = rank_ref[...].T

    def row_base(b):
        return pl.multiple_of(b * cap, 16)

    @pl.when(f == 0)
    def _():
        rt = rankt_ref[pl.ds(e, 1), :]
        hn = hn_ref[...]

        def gather_blk(b, c):
            xg = jnp.zeros((cap, D_MODEL), F32)
            for k0 in range(0, tm, MOE_SEL_CHUNK):
                rid = (lax.broadcasted_iota(jnp.int32, (cap, MOE_SEL_CHUNK), 0) + (b * cap + 1)).astype(F32)
                sel = jnp.where(rt[:, k0:k0 + MOE_SEL_CHUNK] == rid, 1.0, 0.0).astype(BF16)
                xg = xg + jnp.dot(sel, hn[k0:k0 + MOE_SEL_CHUNK, :], preferred_element_type=F32)
            xg_ref[pl.ds(row_base(b), cap), :] = xg.astype(BF16)
            acc_ref[pl.ds(row_base(b), cap), :] = jnp.zeros((cap, D_MODEL), F32)
            return c

        lax.fori_loop(0, nb, gather_blk, 0)

    def ffn_blk(b, c):
        x = xg_ref[pl.ds(row_base(b), cap), :]
        a = jnp.dot(x, wg_ref[0], preferred_element_type=F32)
        u = jnp.dot(x, wu_ref[0], preferred_element_type=F32)
        acc_ref[pl.ds(row_base(b), cap), :] += jnp.dot((_silu(a) * u).astype(BF16), wd_ref[0],
                                                       preferred_element_type=F32)
        return c

    lax.fori_loop(0, nb, ffn_blk, 0)

    @pl.when(f == nf - 1)
    def _():
        ri = ri_ref[...]
        we = jnp.where(ri[:, 0:1] == ef, ri[:, 2:3], 0.0) + jnp.where(ri[:, 1:2] == ef, ri[:, 3:4], 0.0)
        lane = lax.broadcasted_iota(jnp.int32, (tm, LANES), 1)
        rk = jnp.sum(jnp.where(lane == e, rank_ref[...], 0.0), axis=-1, keepdims=True)

        def scatter_blk(b, c):
            yb = acc_ref[pl.ds(row_base(b), cap), :].astype(BF16)
            for r0 in range(0, tm, MOE_SEL_CHUNK):
                cid = (lax.broadcasted_iota(jnp.int32, (MOE_SEL_CHUNK, cap), 1) + (b * cap + 1)).astype(F32)
                sel_t = jnp.where(rk[r0:r0 + MOE_SEL_CHUNK, :] == cid, 1.0, 0.0).astype(BF16)
                y = jnp.dot(sel_t, yb, preferred_element_type=F32)
                o_ref[r0:r0 + MOE_SEL_CHUNK, :] += we[r0:r0 + MOE_SEL_CHUNK, :] * y
            return c

        lax.fori_loop(0, nb, scatter_blk, 0)

        if final:
            @pl.when(e == ne - 1)
            def _():
                o_ref[...] = _rms(o_ref[...], fg_ref[...])


def _moe(h, g, rinfo, nblk, wg, wu, wd, fg, final, tf=896):
    m = h.shape[0]
    d = D_MODEL
    tm = min(MOE_TM, m)
    cap = MOE_CAP
    ne, _, fe = wg.shape
    max_rows = -(-tm // cap) * cap
    grid_spec = pltpu.PrefetchScalarGridSpec(
        num_scalar_prefetch=1,
        grid=(m // tm, ne, fe // tf),
        in_specs=[
            pl.BlockSpec((tm, d), lambda i, e, f, nb: (i, 0)),
            pl.BlockSpec((1, d), lambda i, e, f, nb: (0, 0)),
            pl.BlockSpec((tm, LANES), lambda i, e, f, nb: (i, 0)),
            pl.BlockSpec((1, d, tf), lambda i, e, f, nb: (e, 0, f)),
            pl.BlockSpec((1, d, tf), lambda i, e, f, nb: (e, 0, f)),
            pl.BlockSpec((1, tf, d), lambda i, e, f, nb: (e, f, 0)),
            pl.BlockSpec((1, d), lambda i, e, f, nb: (0, 0)),
        ],
        out_specs=pl.BlockSpec((tm, d), lambda i, e, f, nb: (i, 0)),
        scratch_shapes=[
            pltpu.VMEM((tm, d), BF16),
            pltpu.VMEM((tm, LANES), F32),
            pltpu.VMEM((LANES, tm), F32),
            pltpu.VMEM((max_rows, d), BF16),
            pltpu.VMEM((max_rows, d), F32),
        ],
    )
    return pl.pallas_call(
        functools.partial(_moe_kernel, final=final, cap=cap),
        grid_spec=grid_spec,
        out_shape=jax.ShapeDtypeStruct((m, d), F32),
        compiler_params=pltpu.CompilerParams(dimension_semantics=("parallel", "arbitrary", "arbitrary"),
                                             vmem_limit_bytes=MOE_VMEM_LIMIT),
        name="moe_experts",
    )(nblk, h, g, rinfo, wg, wu, wd, fg)


def _moe_block_counts(rinfo, tm, cap):
    m = rinfo.shape[0]
    ids = rinfo[:, 0:2].astype(jnp.int32)
    onehot = (ids[:, :, None] == jnp.arange(N_EXPERTS, dtype=jnp.int32)).any(axis=1)
    counts = onehot.reshape(m // tm, tm, N_EXPERTS).astype(jnp.int32).sum(axis=1)
    return ((counts + cap - 1) // cap).reshape(-1)


def _final_norm_kernel(h_ref, g_ref, o_ref):
    o_ref[...] = _rms(h_ref[...], g_ref[...])


def _final_norm(h, g, tm=1024):
    m, d = h.shape
    return pl.pallas_call(
        _final_norm_kernel,
        grid=(m // tm,),
        in_specs=[pl.BlockSpec((tm, d), lambda i: (i, 0)), pl.BlockSpec((1, d), lambda i: (0, 0))],
        out_specs=pl.BlockSpec((tm, d), lambda i: (i, 0)),
        out_shape=jax.ShapeDtypeStruct((m, d), F32),
        compiler_params=_cparams("parallel"),
        name="final_norm",
    )(h, g)


def _pack_w_in(w):
    z = w[:, 0:1024]
    xs = w[:, 1024:2048]
    bc = w[:, 2048:2560]
    dt = w[:, 2560:2576]
    lg = w[:, 2576:3600]
    lx = w[:, 3600:4624]
    s5 = w[:, 4624:5648]
    gates = w[:, 5648:8720]
    packed = jnp.concatenate([s5, z, xs, lg, lx, gates, bc], axis=1).astype(BF16)
    wdt = jnp.pad(dt, ((0, 0), (0, LANES - SSD_HEADS))).astype(BF16)
    return packed, wdt


def _block_diag_gates(w_a, w_x, bw=256):
    per = bw // LRU_HEAD_DIM
    nblk = LRU_WIDTH // bw
    eye = jnp.eye(per, dtype=F32)

    def bd(w):
        w = w.reshape(nblk, per, LRU_HEAD_DIM, LRU_HEAD_DIM)
        return jnp.einsum("sbij,bc->sbicj", w, eye).reshape(nblk, bw, bw)

    return jnp.concatenate([bd(w_a), bd(w_x)], axis=2).astype(BF16)


def _row(v, width=None):
    v = v.reshape(1, -1).astype(F32)
    if width is not None and v.shape[1] < width:
        v = jnp.pad(v, ((0, 0), (0, width - v.shape[1])))
    return v


def kernel(x, mix_norm, w_in, ssd_conv_w, ssd_conv_b, ssd_dt_bias, ssd_a_log, ssd_d, ssd_norm, w_branch_ssd, lru_conv_w, lru_conv_b, lru_w_a, lru_b_a, lru_w_x, lru_b_x, lru_lambda, w_branch_lru, s5_lam_re, s5_lam_im, s5_b_re, s5_b_im, s5_c_re, s5_c_im, s5_d, s5_log_dt, w_branch_s5, w_out, ffn_norm, dense_w_gate, dense_w_up, dense_w_down, moe_router, moe_router_bias, moe_w_gate, moe_w_up, moe_w_down, final_norm):
    batch, seq, d = x.shape
    depth = mix_norm.shape[0]
    m = batch * seq
    nchunk5 = seq // S5_CHUNK
    h = x.reshape(m, d)
    normed = False
    for layer in range(depth):
        w_pack, w_dt = _pack_w_in(w_in[layer])
        proj, u5, dt_raw = _inproj(h, _row(mix_norm[layer]), w_pack, w_dt)

        cw = ssd_conv_w[layer]
        cb = ssd_conv_b[layer]
        alog = jnp.pad(ssd_a_log[layer].reshape(1, -1), ((0, 0), (0, LANES - SSD_HEADS)), constant_values=-jnp.inf)
        y_ssd = _ssd(proj, dt_raw, cw[:, :SSD_INNER], _row(cb[:SSD_INNER]), cw[:, SSD_INNER:], _row(cb[SSD_INNER:]),
                     _row(ssd_dt_bias[layer], LANES), alog, _row(jnp.repeat(ssd_d[layer], SSD_HEAD_DIM)),
                     _row(ssd_norm[layer]), batch, seq)

        y_lru = _lru(proj, lru_conv_w[layer], _row(lru_conv_b[layer]),
                     _block_diag_gates(lru_w_a[layer], lru_w_x[layer]), _row(lru_b_a[layer]),
                     _row(lru_b_x[layer]), _row(lru_lambda[layer]), batch, seq)

        wc5, wo5, ar, ai, dflat = _s5_matrices(
            s5_lam_re[layer], s5_lam_im[layer], s5_b_re[layer], s5_b_im[layer], s5_c_re[layer], s5_c_im[layer],
            s5_d[layer], s5_log_dt[layer], nchunk5)
        y_s5 = _s5(u5, wc5, wo5, ar, ai, dflat, nchunk5)

        h = _merge(h, y_ssd, y_lru, y_s5, proj, w_branch_ssd[layer].astype(BF16), w_branch_lru[layer].astype(BF16),
                   w_branch_s5[layer].astype(BF16), w_out[layer].astype(BF16))

        j = layer // 2
        if layer % 2 == 0:
            h = _ffn(h, _row(ffn_norm[layer]), dense_w_gate[j].astype(BF16), dense_w_up[j].astype(BF16),
                     dense_w_down[j].astype(BF16))
        else:
            gn = _row(ffn_norm[layer])
            wr = jnp.pad(moe_router[j], ((0, 0), (0, LANES - N_EXPERTS)))
            rinfo = _router(h, gn, wr, _row(moe_router_bias[j], LANES))
            nblk = _moe_block_counts(rinfo, min(MOE_TM, m), MOE_CAP)
            normed = layer == depth - 1
            h = _moe(h, gn, rinfo, nblk, moe_w_gate[j].astype(BF16), moe_w_up[j].astype(BF16),
                     moe_w_down[j].astype(BF16), _row(final_norm), normed)
    if not normed:
        h = _final_norm(h, _row(final_norm))
    return h.reshape(batch, seq, d)
```
